```python
import math
import jax
import jax.numpy as jnp
from jax import lax
import numpy as np

D_MODEL = 2048
BATCH = 2
SEQ = 4096
DEPTH = 2

GRID_W = 64
CTX_LEN = 256

GLA_HEADS = 4
GLA_DK = 64
GLA_DV = 128
GLA_GATE_RANK = 16
GLA_GATE_NORM = 16.0
GDN_HEADS = 6
GDN_DK = 128
GDN_DV = 128
GDN_CONV = 5
DIFF_HEADS = 6
DIFF_DH = 64
DIFF_DV = 2 * DIFF_DH
DIFF_EPS = 1e-5
ROPE_THETA = 10000.0
D_MIX = GLA_HEADS * GLA_DV + GDN_HEADS * GDN_DV + DIFF_HEADS * DIFF_DV
CHUNK = 64
Q_BLOCK = 128
IN_SPLITS = (
    GLA_HEADS * GLA_DK, GLA_HEADS * GLA_DK, GLA_HEADS * GLA_DV, GLA_HEADS * GLA_DV, 2 * GLA_GATE_RANK,
    GDN_HEADS * GDN_DK, GDN_HEADS * GDN_DK, GDN_HEADS * GDN_DV, GDN_HEADS * GDN_DV, 2 * GDN_HEADS, 2 * GDN_HEADS,
    DIFF_HEADS * 2 * DIFF_DH, DIFF_HEADS * 2 * DIFF_DH, DIFF_HEADS * DIFF_DV,
)
D_IN_PROJ = sum(IN_SPLITS)
GDN_QKV = GDN_HEADS * (2 * GDN_DK + GDN_DV)
N_EXPERTS = 32
TOP_K = 4
D_EXPERT = D_MODEL
SWIGLU_ALPHA = 1.702
SWIGLU_LIMIT = 7.0
MOE_BLOCK = 128
NORM_EPS = 1e-6

kernel_name = "hybrid_gla_gdn_diffattn_moe_dit"

F32 = jnp.float32


def _rmsnorm(x, w, eps=NORM_EPS):
    xf = x.astype(F32)
    y = xf * lax.rsqrt(jnp.mean(xf * xf, axis=-1, keepdims=True) + eps)
    return (y * w.astype(F32)).astype(x.dtype)


def _l2norm(x, eps=1e-6):
    return x * lax.rsqrt(jnp.sum(x * x, axis=-1, keepdims=True) + eps)


def _centred_dwconv(x, w):
    pad = (GDN_CONV - 1) // 2
    return lax.conv_general_dilated(
        x, w[:, None, :].astype(x.dtype), window_strides=(1,), padding=[(pad, pad)],
        dimension_numbers=("NWC", "WIO", "NWC"), feature_group_count=x.shape[-1])


def _axial_rope(x):
    n = x.shape[1]
    rows = n // GRID_W
    row = jnp.repeat(jnp.arange(rows, dtype=F32), GRID_W)
    col = jnp.tile(jnp.arange(GRID_W, dtype=F32), rows)
    half = DIFF_DH // 2
    inv = 1.0 / (ROPE_THETA ** (jnp.arange(0, half, 2, dtype=F32) / half))

    def rot(xa, pos):
        ang = pos[:, None] * inv[None, :]
        cos = jnp.cos(ang)[None, :, None, None, :]
        sin = jnp.sin(ang)[None, :, None, None, :]
        x1, x2 = xa[..., : half // 2], xa[..., half // 2:]
        return jnp.concatenate([x1 * cos - x2 * sin, x2 * cos + x1 * sin], axis=-1)

    xf = x.astype(F32)
    return jnp.concatenate([rot(xf[..., :half], row), rot(xf[..., half:], col)], axis=-1)


def _gla_chunked(q, k, v, log_a, s0):
    B, H, T, dk = q.shape
    dv = v.shape[-1]
    n = T // CHUNK
    q = q.reshape(B, H, n, CHUNK, dk)
    k = k.reshape(B, H, n, CHUNK, dk)
    v = v.reshape(B, H, n, CHUNK, dv)
    b = jnp.cumsum(log_a.reshape(B, H, n, CHUNK, dk), axis=3)
    b_last = b[:, :, :, -1:, :]
    qd = q * jnp.exp(b)
    att = jnp.einsum("bhncd,bhnsd->bhncs", qd, k * jnp.exp(-b))
    att = jnp.where(jnp.tril(jnp.ones((CHUNK, CHUNK), bool)), att, 0.0)
    o_intra = jnp.einsum("bhncs,bhnsv->bhncv", att, v)
    u = jnp.einsum("bhncd,bhncv->bhndv", k * jnp.exp(b_last - b), v)
    decay = jnp.exp(b_last[:, :, :, 0, :])

    def step(S, xs):
        u_n, d_n = xs
        return d_n[..., None] * S + u_n, S

    s_fin, s_in = lax.scan(step, s0, (jnp.moveaxis(u, 2, 0), jnp.moveaxis(decay, 2, 0)))
    o_inter = jnp.einsum("bhncd,nbhdv->bhncv", qd, s_in)
    return (o_intra + o_inter).reshape(B, H, T, dv), s_fin


def _gdn_chunked(q, k, v, beta, g, s0):
    B, H, T, dk = q.shape
    dv = v.shape[-1]
    n = T // CHUNK
    q = q.reshape(B, H, n, CHUNK, dk)
    k = k.reshape(B, H, n, CHUNK, dk)
    v = v.reshape(B, H, n, CHUNK, dv)
    beta = beta.reshape(B, H, n, CHUNK)
    g = jnp.cumsum(g.reshape(B, H, n, CHUNK), axis=-1)
    incl = jnp.tril(jnp.ones((CHUNK, CHUNK), bool))
    strict = jnp.tril(jnp.ones((CHUNK, CHUNK), bool), -1)
    decay = jnp.exp(jnp.where(incl, g[..., :, None] - g[..., None, :], -jnp.inf))
    kb = k * beta[..., None]
    vb = v * beta[..., None]
    lower = jnp.where(strict, jnp.einsum("bhncd,bhnsd->bhncs", kb, k) * decay, 0.0)
    eye = jnp.eye(CHUNK, dtype=q.dtype)
    tmat = lax.linalg.triangular_solve(eye + lower, jnp.broadcast_to(eye, lower.shape),
                                       left_side=True, lower=True, unit_diagonal=True)
    u = tmat @ vb
    w = tmat @ (kb * jnp.exp(g)[..., None])
    a_qk = jnp.einsum("bhncd,bhnsd->bhncs", q, k) * decay
    qg = q * jnp.exp(g)[..., None]
    g_last = g[..., -1]
    kd = k * jnp.exp(g_last[..., None] - g)[..., None]

    def step(S, xs):
        w_n, u_n, qg_n, a_n, kd_n, gl_n = xs
        v_new = u_n - w_n @ S
        o = qg_n @ S + a_n @ v_new
        S = S * jnp.exp(gl_n)[..., None, None] + jnp.einsum("bhcd,bhcv->bhdv", kd_n, v_new)
        return S, o

    xs = tuple(jnp.moveaxis(t, 2, 0) for t in (w, u, qg, a_qk, kd, g_last))
    s_fin, o = lax.scan(step, s0, xs)
    return jnp.moveaxis(o, 0, 2).reshape(B, H, T, dv), s_fin


def _flip(args):
    return tuple(jnp.flip(a, axis=2) for a in args)


def _bidirectional(scan_fn, ctx_f, ctx_b, lat_f, lat_b, s0, need_ctx):
    o_cf, s_cf = scan_fn(*ctx_f, s0)
    o_cb, s_cb = scan_fn(*_flip(ctx_b), s0)
    o_lf, _ = scan_fn(*lat_f, s_cf)
    o_lb, _ = scan_fn(*_flip(lat_b), s_cb)
    o_lat = o_lf + jnp.flip(o_lb, axis=2)
    o_ctx = (o_cf + jnp.flip(o_cb, axis=2)) if need_ctx else None
    return o_lat, o_ctx


def _gated_head_norm(o, og, norm_w):
    B, H, T, dv = o.shape
    o = _rmsnorm(o.transpose(0, 2, 1, 3), norm_w)
    return (o * jax.nn.silu(og.astype(F32).reshape(B, T, H, dv))).reshape(B, T, H * dv)


def _gla_prep(parts, gate_w, gate_b):
    q, k, v, og, glr = parts
    B, T, _ = q.shape
    heads = lambda t, d: t.reshape(B, T, GLA_HEADS, d).transpose(0, 2, 1, 3).astype(F32)
    q = heads(q, GLA_DK) * GLA_DK ** -0.5
    k = heads(k, GLA_DK)
    v = heads(v, GLA_DV)
    glr = glr.reshape(B, T, 2, GLA_GATE_RANK).astype(F32)
    la = jax.nn.log_sigmoid(jnp.einsum("btzr,zrk->zbtk", glr, gate_w.astype(F32))
                            + gate_b.astype(F32)[:, None, None, :]) / GLA_GATE_NORM
    la = la.reshape(2, B, T, GLA_HEADS, GLA_DK).transpose(0, 1, 3, 2, 4)
    return (q, k, v, la[0]), (q, k, v, la[1]), og


def _gla_group(parts_lat, parts_ctx, gate_w, gate_b, norm_w, need_ctx):
    lf, lb, og_l = _gla_prep(parts_lat, gate_w, gate_b)
    cf, cb, og_c = _gla_prep(parts_ctx, gate_w, gate_b)
    s0 = jnp.zeros((og_l.shape[0], GLA_HEADS, GLA_DK, GLA_DV), F32)
    o_l, o_c = _bidirectional(_gla_chunked, cf, cb, lf, lb, s0, need_ctx)
    y_l = _gated_head_norm(o_l, og_l, norm_w)
    y_c = _gated_head_norm(o_c, og_c, norm_w) if need_ctx else None
    return y_l, y_c


def _gdn_prep(parts, conv_w, a_log, dt_bias):
    q, k, v, og, blr, alr = parts
    B, T, _ = q.shape
    qkv = jax.nn.silu(_centred_dwconv(jnp.concatenate([q, k, v], axis=-1), conv_w)).astype(F32)
    q, k, v = jnp.split(qkv, [GDN_HEADS * GDN_DK, 2 * GDN_HEADS * GDN_DK], axis=-1)
    heads = lambda t, d: t.reshape(B, T, GDN_HEADS, d).transpose(0, 2, 1, 3)
    q = _l2norm(heads(q, GDN_DK)) * GDN_DK ** -0.5
    k = _l2norm(heads(k, GDN_DK))
    v = heads(v, GDN_DV)
    beta = jax.nn.sigmoid(blr.astype(F32).reshape(B, T, 2, GDN_HEADS)).transpose(2, 0, 3, 1)
    g = (-jnp.exp(a_log.astype(F32))
         * jax.nn.softplus(alr.astype(F32).reshape(B, T, 2, GDN_HEADS) + dt_bias.astype(F32))
         ).transpose(2, 0, 3, 1)
    return (q, k, v, beta[0], g[0]), (q, k, v, beta[1], g[1]), og


def _gdn_group(parts_lat, parts_ctx, conv_w, a_log, dt_bias, norm_w, need_ctx):
    lf, lb, og_l = _gdn_prep(parts_lat, conv_w, a_log, dt_bias)
    cf, cb, og_c = _gdn_prep(parts_ctx, conv_w, a_log, dt_bias)
    s0 = jnp.zeros((og_l.shape[0], GDN_HEADS, GDN_DK, GDN_DV), F32)
    o_l, o_c = _bidirectional(_gdn_chunked, cf, cb, lf, lb, s0, need_ctx)
    y_l = _gated_head_norm(o_l, og_l, norm_w)
    y_c = _gated_head_norm(o_c, og_c, norm_w) if need_ctx else None
    return y_l, y_c


def _diff_attend(q, k, v, lam):
    s = jnp.einsum("bqhmd,bkhmd->bhmqk", q, k) * DIFF_DH ** -0.5
    p = jax.nn.softmax(s, axis=-1)
    w = p[:, :, 0] - lam * p[:, :, 1]
    return jnp.einsum("bhqk,bkhv->bqhv", w, v)


def _diff_group(parts_lat, parts_ctx, lam_vecs, norm_w, lambda_init, need_ctx):
    def heads(parts):
        q, k, v = parts
        B, T, _ = q.shape
        return (q.reshape(B, T, DIFF_HEADS, 2, DIFF_DH).astype(F32),
                k.reshape(B, T, DIFF_HEADS, 2, DIFF_DH).astype(F32),
                v.reshape(B, T, DIFF_HEADS, DIFF_DV).astype(F32))

    q_l, k_l, v_l = heads(parts_lat)
    q_c, k_c, v_c = heads(parts_ctx)
    B, S = q_l.shape[:2]
    q_l = _axial_rope(q_l)
    k_l = _axial_rope(k_l)
    lf = lam_vecs.astype(F32)
    lam = jnp.exp(jnp.sum(lf[0] * lf[1])) - jnp.exp(jnp.sum(lf[2] * lf[3])) + lambda_init
    k_all = jnp.concatenate([k_l, k_c], axis=1)
    v_all = jnp.concatenate([v_l, v_c], axis=1)
    nb = S // Q_BLOCK
    qb = jnp.moveaxis(q_l.reshape(B, nb, Q_BLOCK, DIFF_HEADS, 2, DIFF_DH), 1, 0)
    ob = lax.map(lambda blk: _diff_attend(blk, k_all, v_all, lam), qb)
    o_l = jnp.moveaxis(ob, 0, 1).reshape(B, S, DIFF_HEADS, DIFF_DV)
    post = lambda o: (_rmsnorm(o, norm_w, DIFF_EPS) * (1.0 - lambda_init)).reshape(o.shape[0], o.shape[1], -1)
    y_l = post(o_l)
    y_c = post(_diff_attend(q_c, k_c, v_c, lam)) if need_ctx else None
    return y_l, y_c


def _head_group_mixer(h_lat, h_ctx, w_in, gla_gate_w, gla_gate_b, gla_norm_w, gdn_conv_w,
                      gdn_a_log, gdn_dt_bias, gdn_norm_w, diff_lambda, diff_norm_w, w_out,
                      lambda_init, need_ctx):
    cuts = [int(v) for v in np.cumsum(IN_SPLITS)[:-1]]
    pl = jnp.split(h_lat @ w_in, cuts, axis=-1)
    pc = jnp.split(h_ctx @ w_in, cuts, axis=-1)
    a_l, a_c = _gla_group(pl[0:5], pc[0:5], gla_gate_w, gla_gate_b, gla_norm_w, need_ctx)
    b_l, b_c = _gdn_group(pl[5:11], pc[5:11], gdn_conv_w, gdn_a_log, gdn_dt_bias, gdn_norm_w, need_ctx)
    d_l, d_c = _diff_group(pl[11:14], pc[11:14], diff_lambda, diff_norm_w, lambda_init, need_ctx)
    out_l = jnp.concatenate([a_l, b_l, d_l], axis=-1).astype(h_lat.dtype) @ w_out
    out_c = (jnp.concatenate([a_c, b_c, d_c], axis=-1).astype(h_ctx.dtype) @ w_out) if need_ctx else None
    return out_l, out_c


def _moe(h, router_w, router_b, w1, b1, w2, b2):
    n_tok, d = h.shape
    logits = h.astype(F32) @ router_w.astype(F32) + router_b.astype(F32)
    top_logit, top_e = lax.top_k(logits, TOP_K)
    gate = jax.nn.softmax(top_logit, axis=-1)
    n_assign = n_tok * TOP_K
    flat_e = top_e.reshape(-1)
    flat_tok = jnp.arange(n_assign, dtype=jnp.int32) // TOP_K
    flat_gate = gate.reshape(-1)
    order = jnp.argsort(flat_e)
    sorted_e = flat_e[order]
    counts = jnp.bincount(flat_e, length=N_EXPERTS)
    padded = (counts + MOE_BLOCK - 1) // MOE_BLOCK * MOE_BLOCK
    start = jnp.cumsum(counts) - counts
    pad_end = jnp.cumsum(padded)
    pad_start = pad_end - padded
    dest = pad_start[sorted_e] + jnp.arange(n_assign, dtype=jnp.int32) - start[sorted_e]
    n_rows = -(-(n_assign + N_EXPERTS * (MOE_BLOCK - 1)) // MOE_BLOCK) * MOE_BLOCK
    n_blocks = n_rows // MOE_BLOCK
    row_tok = jnp.full((n_rows,), n_tok, jnp.int32).at[dest].set(flat_tok[order])
    row_gate = jnp.zeros((n_rows,), F32).at[dest].set(flat_gate[order])
    block_e = jnp.minimum(jnp.searchsorted(pad_end, jnp.arange(n_blocks) * MOE_BLOCK, side="right"),
                          N_EXPERTS - 1)
    h_pad = jnp.concatenate([h, jnp.zeros((1, d), h.dtype)], axis=0)
    xb = h_pad[row_tok].reshape(n_blocks, MOE_BLOCK, d)

    def expert_block(args):
        xe, e = args
        hid = xe @ w1[e] + b1[e]
        x_glu = jnp.minimum(hid[:, ::2], SWIGLU_LIMIT)
        x_lin = jnp.clip(hid[:, 1::2], -SWIGLU_LIMIT, SWIGLU_LIMIT)
        act = x_glu * jax.nn.sigmoid(SWIGLU_ALPHA * x_glu) * (x_lin + 1.0)
        return act @ w2[e] + b2[e]

    yb = lax.map(expert_block, (xb, block_e))
    y = yb.reshape(n_rows, d) * row_gate[:, None].astype(h.dtype)
    return jnp.zeros_like(h_pad).at[row_tok].add(y)[:n_tok]


def setup_inputs(seed: int = 0) -> dict:
    key = jax.random.key(seed)
    ks = jax.random.split(key, 32)
    D = D_MODEL
    nrm = lambda k, shape, s: jax.random.normal(k, shape, F32) * s
    dt = jnp.exp(jax.random.uniform(ks[13], (DEPTH, 2, GDN_HEADS), F32, math.log(1e-3), math.log(1e-1)))
    return {
        "x": nrm(ks[0], (BATCH, SEQ, D), 1.0),
        "c": nrm(ks[1], (BATCH, D), 1.0),
        "ctx": nrm(ks[2], (BATCH, CTX_LEN, D), 1.0),
        "c_ctx": nrm(ks[3], (D,), 1.0),
        "w_mod": nrm(ks[4], (DEPTH, D, 6 * D), 0.5 * D ** -0.5),
        "b_mod": nrm(ks[5], (DEPTH, 6 * D), 0.02),
        "norm1_w": 1.0 + nrm(ks[6], (DEPTH, D), 0.1),
        "w_in": nrm(ks[7], (DEPTH, D, D_IN_PROJ), D ** -0.5),
        "gla_gate_w": nrm(ks[8], (DEPTH, 2, GLA_GATE_RANK, GLA_HEADS * GLA_DK), GLA_GATE_RANK ** -0.5),
        "gla_gate_b": nrm(ks[9], (DEPTH, 2, GLA_HEADS * GLA_DK), 0.1),
        "gla_norm_w": 1.0 + nrm(ks[10], (DEPTH, GLA_DV), 0.1),
        "gdn_conv_w": nrm(ks[11], (DEPTH, GDN_CONV, GDN_QKV), GDN_CONV ** -0.5),
        "gdn_a_log": jnp.log(jax.random.uniform(ks[12], (DEPTH, 2, GDN_HEADS), F32, 1.0, 16.0)),
        "gdn_dt_bias": dt + jnp.log(-jnp.expm1(-dt)),
        "gdn_norm_w": 1.0 + nrm(ks[14], (DEPTH, GDN_DV), 0.1),
        "diff_lambda": nrm(ks[15], (DEPTH, 4, DIFF_DH), 0.1),
        "diff_norm_w": 1.0 + nrm(ks[16], (DEPTH, DIFF_DV), 0.1),
        "w_out": nrm(ks[17], (DEPTH, D_MIX, D), D_MIX ** -0.5),
        "norm2_w": 1.0 + nrm(ks[18], (DEPTH, D), 0.1),
        "router_w": nrm(ks[19], (DEPTH, D, N_EXPERTS), D ** -0.5),
        "router_b": nrm(ks[20], (DEPTH, N_EXPERTS), 0.01),
        "expert_w1": nrm(ks[21], (DEPTH, N_EXPERTS, D, 2 * D_EXPERT), D ** -0.5),
        "expert_b1": nrm(ks[22], (DEPTH, N_EXPERTS, 2 * D_EXPERT), 0.01),
        "expert_w2": nrm(ks[23], (DEPTH, N_EXPERTS, D_EXPERT, D), D_EXPERT ** -0.5),
        "expert_b2": nrm(ks[24], (DEPTH, N_EXPERTS, D), 0.01),
        "final_norm_w": 1.0 + nrm(ks[25], (D,), 0.1),
    }


def reference(x, c, ctx, c_ctx, w_mod, b_mod, norm1_w, w_in, gla_gate_w, gla_gate_b, gla_norm_w,
              gdn_conv_w, gdn_a_log, gdn_dt_bias, gdn_norm_w, diff_lambda, diff_norm_w, w_out,
              norm2_w, router_w, router_b, expert_w1, expert_b1, expert_w2, expert_b2, final_norm_w):
    B, S, D = x.shape
    L = ctx.shape[1]
    for l in range(DEPTH):
        last = l == DEPTH - 1
        lambda_init = 0.8 - 0.6 * math.exp(-0.3 * l)
        mod = jax.nn.silu(c) @ w_mod[l] + b_mod[l]
        mod_c = jax.nn.silu(c_ctx) @ w_mod[l] + b_mod[l]
        sh1, sc1, g1, sh2, sc2, g2 = jnp.split(mod, 6, axis=-1)
        sh1c, sc1c, g1c, sh2c, sc2c, g2c = jnp.split(mod_c, 6, axis=-1)
        h_lat = _rmsnorm(x, norm1_w[l]) * (1.0 + sc1[:, None]) + sh1[:, None]
        h_ctx = _rmsnorm(ctx, norm1_w[l]) * (1.0 + sc1c) + sh1c
        a_lat, a_ctx = _head_group_mixer(
            h_lat, h_ctx, w_in[l], gla_gate_w[l], gla_gate_b[l], gla_norm_w[l], gdn_conv_w[l],
            gdn_a_log[l], gdn_dt_bias[l], gdn_norm_w[l], diff_lambda[l], diff_norm_w[l], w_out[l],
            lambda_init, not last)
        x = x + g1[:, None] * a_lat
        h2 = _rmsnorm(x, norm2_w[l]) * (1.0 + sc2[:, None]) + sh2[:, None]
        if last:
            y = _moe(h2.reshape(B * S, D), router_w[l], router_b[l], expert_w1[l], expert_b1[l],
                     expert_w2[l], expert_b2[l]).reshape(B, S, D)
            x = x + g2[:, None] * y
        else:
            ctx = ctx + g1c * a_ctx
            h2c = _rmsnorm(ctx, norm2_w[l]) * (1.0 + sc2c) + sh2c
            tokens = jnp.concatenate([h2.reshape(B * S, D), h2c.reshape(B * L, D)], axis=0)
            y_all = _moe(tokens, router_w[l], router_b[l], expert_w1[l], expert_b1[l],
                         expert_w2[l], expert_b2[l])
            x = x + g2[:, None] * y_all[: B * S].reshape(B, S, D)
            ctx = ctx + g2c * y_all[B * S:].reshape(B, L, D)
    return _rmsnorm(x, final_norm_w)
```

```python
import functools
import math

import jax
import jax.numpy as jnp
from jax import lax
from jax.experimental import pallas as pl
from jax.experimental.pallas import tpu as pltpu

F32 = jnp.float32
BF16 = jnp.bfloat16

GRID_W = 64
GLA_HEADS, GLA_DK, GLA_DV, GLA_RANK, GLA_GATE_NORM = 4, 64, 128, 16, 16.0
GDN_HEADS, GDN_DK, GDN_DV, GDN_CONV = 6, 128, 128, 5
DIFF_HEADS, DIFF_DH, DIFF_DV, DIFF_EPS = 6, 64, 128, 1e-5
ROPE_THETA = 10000.0
CHUNK = 64
N_EXPERTS, TOP_K = 32, 4
SWIGLU_ALPHA, SWIGLU_LIMIT = 1.702, 7.0
NORM_EPS = 1e-6
LANES = 128
MOD_ROWS = 8

A_Q, A_K, A_V, A_OG = 0, 256, 512, 1024
B_Q, B_K, B_V, B_OG = 1536, 2304, 3072, 3840
C_Q, C_K, C_V = 4608, 5376, 6144
A_GLR, B_BA = 6912, 7040
NP_IN = 7168
D_MIX = GLA_HEADS * GLA_DV + GDN_HEADS * GDN_DV + DIFF_HEADS * DIFF_DV

VMEM_LIMIT = 56 * 1024 * 1024

HI = lax.Precision.HIGHEST


def _dot(a, b):
    return jnp.dot(a, b, preferred_element_type=F32)


def _dot_hi(a, b):
    return jnp.dot(a, b, preferred_element_type=F32, precision=HI)


def _dot_nt(a, b, precision=None):
    return lax.dot_general(a, b, (((1,), (1,)), ((), ())), preferred_element_type=F32, precision=precision)


def _pick(n, pref):
    if n <= pref:
        return n
    t = pref - pref % LANES
    while t > LANES and n % t:
        t -= LANES
    return t


def _silu(x):
    return x * jax.nn.sigmoid(x)


def _log_sigmoid(x):
    return jnp.minimum(x, 0.0) - jnp.log(1.0 + jnp.exp(-jnp.abs(x)))


def _softplus(x):
    return jnp.maximum(x, 0.0) + jnp.log(1.0 + jnp.exp(-jnp.abs(x)))


def _cparams(sem):
    return pltpu.CompilerParams(dimension_semantics=sem, vmem_limit_bytes=VMEM_LIMIT)


def _mod_kernel(c_ref, w_ref, b_ref, o_ref):
    a = _silu(c_ref[...])
    o_ref[0] = _dot(a.astype(BF16), w_ref[0].astype(BF16)) + b_ref[0]


def _modulation(cvec, w_mod, b_mod):
    depth, d, n6 = w_mod.shape
    tn = _pick(n6, 1024)
    return pl.pallas_call(
        _mod_kernel,
        out_shape=jax.ShapeDtypeStruct((depth, MOD_ROWS, n6), F32),
        grid=(depth, n6 // tn),
        in_specs=[pl.BlockSpec((MOD_ROWS, d), lambda l, j: (0, 0)),
                  pl.BlockSpec((1, d, tn), lambda l, j: (l, 0, j)),
                  pl.BlockSpec((1, 1, tn), lambda l, j: (l, 0, j))],
        out_specs=pl.BlockSpec((1, MOD_ROWS, tn), lambda l, j: (l, 0, j)),
        compiler_params=_cparams(("parallel", "parallel")),
        name="modulation",
    )(cvec, w_mod, b_mod.reshape(depth, 1, n6))


def _group_of(sub_block, subs_per_batch, n_batch):
    return jnp.where(sub_block % subs_per_batch == 0, n_batch, sub_block // subs_per_batch)


def _rms(x, eps):
    return x * lax.rsqrt(jnp.mean(x * x, axis=-1, keepdims=True) + eps)


def _in_proj_kernel(x_ref, nw_ref, mod_ref, w_ref, o_ref, h_scr, *, sub, subs_per_batch, n_batch):
    i = pl.program_id(0)

    @pl.when(pl.program_id(1) == 0)
    def _():
        for s in range(x_ref.shape[0] // sub):
            m = mod_ref[_group_of(i * (x_ref.shape[0] // sub) + s, subs_per_batch, n_batch)]
            y = _rms(x_ref[s * sub:(s + 1) * sub, :], NORM_EPS) * nw_ref[...]
            h_scr[s * sub:(s + 1) * sub, :] = (y * (1.0 + m[1:2]) + m[0:1]).astype(BF16)

    o_ref[...] = _dot(h_scr[...], w_ref[...])


def _in_proj(xt, norm_w, mod, w_pad, *, sub, subs_per_batch, n_batch):
    n, d = xt.shape
    npad = w_pad.shape[1]
    tm = sub * 2 if (n // sub) % 2 == 0 else sub
    tn = _pick(npad, 512)
    kern = functools.partial(_in_proj_kernel, sub=sub, subs_per_batch=subs_per_batch, n_batch=n_batch)
    return pl.pallas_call(
        kern,
        out_shape=jax.ShapeDtypeStruct((n, npad), F32),
        grid=(n // tm, npad // tn),
        in_specs=[pl.BlockSpec((tm, d), lambda i, j: (i, 0)),
                  pl.BlockSpec((1, d), lambda i, j: (0, 0)),
                  pl.BlockSpec((MOD_ROWS, 6, d), lambda i, j: (0, 0, 0)),
                  pl.BlockSpec((d, tn), lambda i, j: (0, j))],
        out_specs=pl.BlockSpec((tm, tn), lambda i, j: (i, j)),
        scratch_shapes=[pltpu.VMEM((tm, d), BF16)],
        compiler_params=_cparams(("parallel", "arbitrary")),
        name="in_proj",
    )(xt, norm_w.reshape(1, d), mod, w_pad)


def _tri_masks(c):
    r = lax.broadcasted_iota(jnp.int32, (c, c), 0)
    s = lax.broadcasted_iota(jnp.int32, (c, c), 1)
    return ((s <= r, s < r), (s >= r, s > r))


def _bwd_chunk(i, nc_ctx, nc):
    return jnp.where(i < nc_ctx, nc_ctx - 1 - i, nc - 1 - (i - nc_ctx))


def _gla_kernel(q_ref, k_ref, v_ref, glr_ref, gw_ref, gb_ref, o_ref, st_scr, *, nc_ctx):
    t_len = q_ref.shape[0]
    nc = t_len // CHUNK
    masks = _tri_masks(CHUNK)
    lane = lax.broadcasted_iota(jnp.int32, (1, LANES), 1)
    head_mask = [(lane < GLA_DK).astype(F32), (lane >= GLA_DK).astype(F32)]
    o_ref[...] = jnp.zeros(o_ref.shape, F32)
    st_scr[...] = jnp.zeros(st_scr.shape, F32)

    def step(i, carry):
        for z in range(2):
            c = i if z == 0 else _bwd_chunk(i, nc_ctx, nc)
            rows = pl.ds(pl.multiple_of(c * CHUNK, CHUNK), CHUNK)
            incl, _ = masks[z]
            q = q_ref[rows, :] * GLA_DK ** -0.5
            k = k_ref[rows, :]
            pre = _dot_hi(glr_ref[rows, :], gw_ref[z]) + gb_ref[z]
            la = _log_sigmoid(pre) / GLA_GATE_NORM
            b = _dot_hi(incl.astype(F32), la)
            tot = b[CHUNK - 1:CHUNK, :] if z == 0 else b[0:1, :]
            qd = q * jnp.exp(b)
            kk = (k * jnp.exp(-b)).astype(BF16)
            kd = k * jnp.exp(tot - b)
            dec = jnp.exp(tot)
            for j in range(2):
                qdj = (qd * head_mask[j]).astype(BF16)
                att = jnp.where(incl, _dot_nt(qdj, kk), 0.0)
                vj = v_ref[rows, j * GLA_DV:(j + 1) * GLA_DV]
                st = st_scr[z, j]
                o = _dot(att.astype(BF16), vj.astype(BF16)) + _dot_nt(qdj, st.astype(BF16))
                o_ref[rows, j * GLA_DV:(j + 1) * GLA_DV] += o
                st_scr[z, j] = st * dec + _dot(vj.T.astype(BF16), (kd * head_mask[j]).astype(BF16))
        return carry

    lax.fori_loop(0, nc, step, 0)


def _gla(p, gw_pad, gb, *, n_batch, t_len, l_ctx):
    n = p.shape[0]
    spec = lambda col, w: pl.BlockSpec((t_len, w), lambda b, hp, col=col, w=w: (b, col // w + hp))
    kern = functools.partial(_gla_kernel, nc_ctx=l_ctx // CHUNK)
    return pl.pallas_call(
        kern,
        out_shape=jax.ShapeDtypeStruct((n, GLA_HEADS * GLA_DV), F32),
        grid=(n_batch, GLA_HEADS // 2),
        in_specs=[spec(A_Q, LANES), spec(A_K, LANES), spec(A_V, 2 * GLA_DV),
                  pl.BlockSpec((t_len, LANES), lambda b, hp: (b, A_GLR // LANES)),
                  pl.BlockSpec((2, LANES, LANES), lambda b, hp: (0, 0, hp)),
                  pl.BlockSpec((2, 1, LANES), lambda b, hp: (0, 0, hp))],
        out_specs=pl.BlockSpec((t_len, 2 * GLA_DV), lambda b, hp: (b, hp)),
        scratch_shapes=[pltpu.VMEM((2, 2, GLA_DV, LANES), F32)],
        compiler_params=_cparams(("parallel", "parallel")),
        name="gla_scan",
    )(p, p, p, p, gw_pad, gb)


def _gdn_kernel(q_ref, k_ref, v_ref, ba_ref, cwq_ref, cwk_ref, cwv_ref, ad_ref, o_ref,
                pad_scr, qn_scr, kn_scr, vn_scr, s_scr, *, nc_ctx, l_ctx):
    t_len = q_ref.shape[0]
    nc = t_len // CHUNK
    h = pl.program_id(1)
    masks = _tri_masks(CHUNK)
    lane = lax.broadcasted_iota(jnp.int32, (1, LANES), 1)
    halo = 8
    pad = (GDN_CONV - 1) // 2

    def conv_pass(src_ref, cw_ref, dst_scr, l2, scale):
        pad_scr[0:halo, :] = jnp.zeros((halo, LANES), F32)
        pad_scr[halo + t_len:2 * halo + t_len, :] = jnp.zeros((halo, LANES), F32)
        pad_scr[halo:halo + t_len, :] = src_ref[...]

        def body(c, carry):
            r0 = pl.multiple_of(c * CHUNK, CHUNK)
            win = pad_scr[pl.ds(r0, CHUNK + 2 * halo), :]
            row = r0 + lax.broadcasted_iota(jnp.int32, (CHUNK, 1), 0)
            acc = jnp.zeros((CHUNK, LANES), F32)
            for j in range(GDN_CONV):
                d = j - pad
                tap = win[halo + d:halo + d + CHUNK, :]
                same_seq = ((row + d) >= l_ctx) == (row >= l_ctx)
                acc = acc + jnp.where(same_seq, tap, 0.0) * cw_ref[j:j + 1, :]
            y = _silu(acc)
            if l2:
                y = y * lax.rsqrt(jnp.sum(y * y, axis=-1, keepdims=True) + 1e-6) * scale
            dst_scr[pl.ds(r0, CHUNK), :] = y
            return carry

        lax.fori_loop(0, nc, body, 0)

    conv_pass(q_ref, cwq_ref, qn_scr, True, GDN_DK ** -0.5)
    conv_pass(k_ref, cwk_ref, kn_scr, True, 1.0)
    conv_pass(v_ref, cwv_ref, vn_scr, False, 1.0)

    o_ref[...] = jnp.zeros(o_ref.shape, F32)
    s_scr[...] = jnp.zeros(s_scr.shape, F32)
    ones_l0 = jnp.broadcast_to((lane == 0).astype(F32), (CHUNK, LANES))

    def step(i, carry):
        for z in range(2):
            c = i if z == 0 else _bwd_chunk(i, nc_ctx, nc)
            rows = pl.ds(pl.multiple_of(c * CHUNK, CHUNK), CHUNK)
            incl, strict = masks[z]
            q, k, v = qn_scr[rows, :], kn_scr[rows, :], vn_scr[rows, :]
            ba = ba_ref[rows, :]
            beta_all = jax.nn.sigmoid(ba)
            g_all = -jnp.exp(ad_ref[0:1, :]) * _softplus(ba + ad_ref[1:2, :])
            col = z * GDN_HEADS + h
            beta = jnp.sum(jnp.where(lane == col, beta_all, 0.0), axis=-1, keepdims=True)
            g = jnp.sum(jnp.where(lane == 2 * GDN_HEADS + col, g_all, 0.0), axis=-1, keepdims=True)
            gc = _dot_hi(incl.astype(F32), jnp.broadcast_to(g, (CHUNK, LANES)))
            grow = _dot_nt(ones_l0, gc, precision=HI)[:, :CHUNK]
            dmat = jnp.exp(jnp.where(incl, gc[:, :CHUNK] - grow, -jnp.inf))
            tot = gc[CHUNK - 1:CHUNK, :] if z == 0 else gc[0:1, :]
            kb = k * beta
            kbf = k.astype(BF16)
            nm = -jnp.where(strict, _dot_nt(kb.astype(BF16), kbf) * dmat, 0.0)
            x = jnp.concatenate([v * beta, kb * jnp.exp(gc)], axis=1)
            npow = nm
            for lvl in range(6):
                x = x + _dot(npow.astype(BF16), x.astype(BF16))
                if lvl < 5:
                    npow = _dot(npow.astype(BF16), npow.astype(BF16))
            u, w = x[:, :GDN_DV], x[:, GDN_DV:]
            a_qk = jnp.where(incl, _dot_nt(q.astype(BF16), kbf) * dmat, 0.0)
            qg = q * jnp.exp(gc)
            kd = k * jnp.exp(tot - gc)
            s = s_scr[z]
            sb = s.astype(BF16)
            v_new = u - _dot(w.astype(BF16), sb)
            o = _dot(qg.astype(BF16), sb) + _dot(a_qk.astype(BF16), v_new.astype(BF16))
            o_ref[rows, :] += o
            s_scr[z] = s * jnp.exp(tot) + _dot(kd.T.astype(BF16), v_new.astype(BF16))
        return carry

    lax.fori_loop(0, nc, step, 0)


def _gdn(p, conv_w, ad, *, n_batch, t_len, l_ctx):
    n = p.shape[0]
    spec = lambda col: pl.BlockSpec((t_len, LANES), lambda b, h, col=col: (b, col // LANES + h))
    cspec = lambda off: pl.BlockSpec((GDN_CONV, LANES), lambda b, h, off=off: (0, off + h))
    kern = functools.partial(_gdn_kernel, nc_ctx=l_ctx // CHUNK, l_ctx=l_ctx)
    return pl.pallas_call(
        kern,
        out_shape=jax.ShapeDtypeStruct((n, GDN_HEADS * GDN_DV), F32),
        grid=(n_batch, GDN_HEADS),
        in_specs=[spec(B_Q), spec(B_K), spec(B_V),
                  pl.BlockSpec((t_len, LANES), lambda b, h: (b, B_BA // LANES)),
                  cspec(0), cspec(GDN_HEADS), cspec(2 * GDN_HEADS),
                  pl.BlockSpec((8, LANES), lambda b, h: (0, 0))],
        out_specs=pl.BlockSpec((t_len, GDN_DV), lambda b, h: (b, h)),
        scratch_shapes=[pltpu.VMEM((t_len + 16, LANES), F32),
                        pltpu.VMEM((t_len, LANES), F32),
                        pltpu.VMEM((t_len, LANES), F32),
                        pltpu.VMEM((t_len, LANES), F32),
                        pltpu.VMEM((2, GDN_DK, GDN_DV), F32)],
        compiler_params=_cparams(("parallel", "parallel")),
        name="gdn_scan",
    )(p, p, p, p, conv_w, conv_w, conv_w, ad)


def _rope_kernel(x_ref, cos_ref, sa_ref, sb_ref, o_ref):
    x = x_ref[...]
    scale = jnp.where(pl.program_id(1) < DIFF_HEADS, DIFF_DH ** -0.5, 1.0)
    y = x * cos_ref[...] + pltpu.roll(x, LANES - 16, 1) * sa_ref[...] + pltpu.roll(x, 16, 1) * sb_ref[...]
    o_ref[...] = (y * scale).astype(BF16)


def _rope(p, cos_t, sa_t, sb_t, *, tm):
    n = p.shape[0]
    tblocks = cos_t.shape[0] // tm
    tspec = pl.BlockSpec((tm, LANES), lambda i, j: (i % tblocks, 0))
    return pl.pallas_call(
        _rope_kernel,
        out_shape=jax.ShapeDtypeStruct((n, 2 * DIFF_HEADS * LANES), BF16),
        grid=(n // tm, 2 * DIFF_HEADS),
        in_specs=[pl.BlockSpec((tm, LANES), lambda i, j: (i, C_Q // LANES + j)), tspec, tspec, tspec],
        out_specs=pl.BlockSpec((tm, LANES), lambda i, j: (i, j)),
        compiler_params=_cparams(("parallel", "parallel")),
        name="rope",
    )(p, cos_t, sa_t, sb_t)


def _rope_tables(t_len, l_ctx):
    pos = jnp.arange(t_len - l_ctx)
    row = (pos // GRID_W).astype(F32)
    colp = (pos % GRID_W).astype(F32)
    half = DIFF_DH // 2
    inv = 1.0 / (ROPE_THETA ** (jnp.arange(0, half, 2, dtype=F32) / half))
    lane = jnp.arange(LANES)
    part = (lane % DIFF_DH) // half
    upper = (lane % half) // (half // 2)
    ang = jnp.where(part[None, :] == 0, row[:, None], colp[:, None]) * inv[lane % (half // 2)][None, :]
    cos, sin = jnp.cos(ang), jnp.sin(ang)
    sa = jnp.where(upper[None, :] == 0, -sin, 0.0)
    sb = jnp.where(upper[None, :] == 1, sin, 0.0)
    ident = lambda v, fill: jnp.concatenate([jnp.full((l_ctx, LANES), fill, F32), v.astype(F32)], axis=0)
    return ident(cos, 1.0), ident(sa, 0.0), ident(sb, 0.0)


def _attn_kernel(q_ref, k_ref, v_ref, lam_ref, nw_ref, o_ref, *, l_ctx, lambda_init):
    tq = q_ref.shape[0]
    t_len = k_ref.shape[0]
    lv = lam_ref[...]
    lam = (jnp.exp(jnp.sum(lv[0:1] * lv[1:2], axis=-1, keepdims=True))
           - jnp.exp(jnp.sum(lv[2:3] * lv[3:4], axis=-1, keepdims=True)) + lambda_init)
    lane = lax.broadcasted_iota(jnp.int32, (1, LANES), 1)

    def attend(kv_len):
        q = q_ref[...]
        k = k_ref[0:kv_len, :]
        v = v_ref[0:kv_len, :].astype(BF16)
        probs = []
        for m in range(2):
            qm = jnp.where((lane >= DIFF_DH) == (m == 1), q, jnp.zeros_like(q))
            s = _dot_nt(qm, k)
            e = jnp.exp(s - jnp.max(s, axis=-1, keepdims=True))
            probs.append(e * (1.0 / jnp.sum(e, axis=-1, keepdims=True)))
        w = probs[0] - lam * probs[1]
        o = _dot(w.astype(BF16), v)
        o_ref[...] = _rms(o, DIFF_EPS) * nw_ref[...] * (1.0 - lambda_init)

    is_ctx = pl.program_id(2) < l_ctx // tq
    pl.when(is_ctx)(lambda: attend(l_ctx))
    pl.when(jnp.logical_not(is_ctx))(lambda: attend(t_len))


def _diff_attn(qk, p, lam_vecs, norm_w, *, n_batch, t_len, l_ctx, lambda_init):
    n = p.shape[0]
    tq = min(l_ctx, 256)
    nt = t_len // tq
    kern = functools.partial(_attn_kernel, l_ctx=l_ctx, lambda_init=lambda_init)
    return pl.pallas_call(
        kern,
        out_shape=jax.ShapeDtypeStruct((n, DIFF_HEADS * DIFF_DV), F32),
        grid=(n_batch, DIFF_HEADS, nt),
        in_specs=[pl.BlockSpec((tq, LANES), lambda b, h, t: (b * nt + t, h)),
                  pl.BlockSpec((t_len, LANES), lambda b, h, t: (b, DIFF_HEADS + h)),
                  pl.BlockSpec((t_len, LANES), lambda b, h, t: (b, C_V // LANES + h)),
                  pl.BlockSpec((4, DIFF_DH), lambda b, h, t: (0, 0)),
                  pl.BlockSpec((1, DIFF_DV), lambda b, h, t: (0, 0))],
        out_specs=pl.BlockSpec((tq, DIFF_DV), lambda b, h, t: (b * nt + t, h)),
        compiler_params=_cparams(("parallel", "parallel", "parallel")),
        name="diff_attn",
    )(qk, qk, p, lam_vecs, norm_w.reshape(1, DIFF_DV))


def _out_proj_kernel(oa_ref, oga_ref, ob_ref, ogb_ref, yd_ref, anw_ref, bnw_ref, w_ref, x_ref, mod_ref,
                     n2_ref, rw_ref, rb_ref, xo_ref, h2_ref, te_ref, tg_ref, *, subs_per_batch, n_batch):
    m = mod_ref[_group_of(pl.program_id(0), subs_per_batch, n_batch)]
    pieces = []
    for o_ref, og_ref, nw_ref, heads in ((oa_ref, oga_ref, anw_ref, GLA_HEADS), (ob_ref, ogb_ref, bnw_ref, GDN_HEADS)):
        for hh in range(heads):
            sl = slice(hh * LANES, (hh + 1) * LANES)
            pieces.append((_rms(o_ref[:, sl], NORM_EPS) * nw_ref[...] * _silu(og_ref[:, sl])).astype(BF16))
    pieces.append(yd_ref[...].astype(BF16))
    mix = jnp.concatenate(pieces, axis=1)
    x_new = x_ref[...] + m[2:3] * _dot(mix, w_ref[...])
    xo_ref[...] = x_new
    h2 = _rms(x_new, NORM_EPS) * n2_ref[...] * (1.0 + m[4:5]) + m[3:4]
    h2_ref[...] = h2.astype(BF16)
    lg = _dot_hi(h2, rw_ref[...]) + rb_ref[...]
    lane = lax.broadcasted_iota(jnp.int32, lg.shape, 1)
    lane_f = lane.astype(F32)
    vals, idxs = [], []
    for _ in range(TOP_K):
        mx = jnp.max(lg, axis=-1, keepdims=True)
        ix = jnp.min(jnp.where(lg == mx, lane_f, float(LANES)), axis=-1, keepdims=True)
        vals.append(mx)
        idxs.append(ix)
        lg = jnp.where(lane_f == ix, -jnp.inf, lg)
    ex = [jnp.exp(v - vals[0]) for v in vals]
    inv_den = 1.0 / (ex[0] + ex[1] + ex[2] + ex[3])
    te = jnp.zeros(lg.shape, F32)
    tg = jnp.zeros(lg.shape, F32)
    for kk in range(TOP_K):
        te = jnp.where(lane == kk, idxs[kk], te)
        tg = jnp.where(lane == kk, ex[kk] * inv_den, tg)
    te_ref[...] = te.astype(jnp.int32)
    tg_ref[...] = tg


def _out_proj(oa, ob, yd, p, gla_nw, gdn_nw, w_out, xt, mod, norm2_w, rw_pad, rb_pad, *, tm, subs_per_batch, n_batch):
    n, d = xt.shape
    kern = functools.partial(_out_proj_kernel, subs_per_batch=subs_per_batch, n_batch=n_batch)
    const = lambda shape: pl.BlockSpec(shape, lambda i: (0,) * len(shape))
    wa, wb, wd = GLA_HEADS * GLA_DV, GDN_HEADS * GDN_DV, DIFF_HEADS * DIFF_DV
    return pl.pallas_call(
        kern,
        out_shape=(jax.ShapeDtypeStruct((n, d), F32), jax.ShapeDtypeStruct((n, d), BF16),
                   jax.ShapeDtypeStruct((n, LANES), jnp.int32), jax.ShapeDtypeStruct((n, LANES), F32)),
        grid=(n // tm,),
        in_specs=[pl.BlockSpec((tm, wa), lambda i: (i, 0)),
                  pl.BlockSpec((tm, wa), lambda i: (i, A_OG // wa)),
                  pl.BlockSpec((tm, wb), lambda i: (i, 0)),
                  pl.BlockSpec((tm, wb), lambda i: (i, B_OG // wb)),
                  pl.BlockSpec((tm, wd), lambda i: (i, 0)),
                  const((1, LANES)), const((1, LANES)), const((D_MIX, d)),
                  pl.BlockSpec((tm, d), lambda i: (i, 0)),
                  const((MOD_ROWS, 6, d)), const((1, d)), const((d, LANES)), const((1, LANES))],
        out_specs=(pl.BlockSpec((tm, d), lambda i: (i, 0)), pl.BlockSpec((tm, d), lambda i: (i, 0)),
                   pl.BlockSpec((tm, LANES), lambda i: (i, 0)), pl.BlockSpec((tm, LANES), lambda i: (i, 0))),
        compiler_params=_cparams(("parallel",)),
        name="out_proj",
    )(oa, p, ob, p, yd, gla_nw.reshape(1, LANES), gdn_nw.reshape(1, LANES), w_out, xt, mod,
      norm2_w.reshape(1, d), rw_pad, rb_pad)


MOE_SUB = 256
PERM = 256


def _moe1_kernel(ve_ref, vn_ref, nu_ref, x_ref, w_ref, b_ref, perm_ref, o_ref, wp_scr):
    v = pl.program_id(0)
    tn = w_ref.shape[2]

    @pl.when(vn_ref[v] > 0)
    def _():
        for c in range(tn // PERM):
            wc = w_ref[0, :, c * PERM:(c + 1) * PERM].astype(BF16)
            wp_scr[:, c * PERM:(c + 1) * PERM] = _dot(wc, perm_ref[...]).astype(BF16)

    def body(s, carry):
        rows = pl.ds(pl.multiple_of(s * MOE_SUB, MOE_SUB), MOE_SUB)
        hid = _dot(x_ref[rows, :], wp_scr[...]) + b_ref[0]
        for c in range(tn // PERM):
            glu = jnp.minimum(hid[:, c * PERM:c * PERM + PERM // 2], SWIGLU_LIMIT)
            lin = jnp.clip(hid[:, c * PERM + PERM // 2:(c + 1) * PERM], -SWIGLU_LIMIT, SWIGLU_LIMIT)
            act = glu * jax.nn.sigmoid(SWIGLU_ALPHA * glu) * (lin + 1.0)
            o_ref[rows, c * (PERM // 2):(c + 1) * (PERM // 2)] = act.astype(BF16)
        return carry

    lax.fori_loop(0, vn_ref[v], body, 0)


def _moe2_kernel(ve_ref, vn_ref, nu_ref, a_ref, w_ref, b_ref, o_ref, wb_scr):
    v = pl.program_id(0)

    @pl.when(vn_ref[v] > 0)
    def _():
        wb_scr[...] = w_ref[0].astype(BF16)

    def body(s, carry):
        rows = pl.ds(pl.multiple_of(s * MOE_SUB, MOE_SUB), MOE_SUB)
        o_ref[rows, :] = _dot(a_ref[rows, :], wb_scr[...]) + b_ref[0]
        return carry

    lax.fori_loop(0, vn_ref[v], body, 0)


def _moe_experts(xs, vis_e, vis_n, n_used, w1, b1p, w2, b2, perm, *, rows_per_visit):
    n_exp, d, dh2 = w1.shape
    dh = dh2 // 2
    r = rows_per_visit
    n_vis = xs.shape[0] // r

    def maps(n_j):
        live = lambda v, nu: v < nu[0]
        jj = lambda v, j, nu: jnp.where(live(v, nu), j, n_j - 1)
        x_map = lambda v, j, ve, vn, nu: (jnp.minimum(v, nu[0] - 1), 0)
        w_map = lambda v, j, ve, vn, nu: (ve[v], 0, jj(v, j, nu))
        o_map = lambda v, j, ve, vn, nu: (jnp.where(live(v, nu), v, n_vis), jj(v, j, nu))
        return x_map, w_map, o_map

    tn1 = _pick(dh2, 512)
    x_map, w_map, o_map = maps(dh2 // tn1)
    act = pl.pallas_call(
        _moe1_kernel,
        out_shape=jax.ShapeDtypeStruct(((n_vis + 1) * r, dh), BF16),
        grid_spec=pltpu.PrefetchScalarGridSpec(
            num_scalar_prefetch=3, grid=(n_vis, dh2 // tn1),
            in_specs=[pl.BlockSpec((r, d), x_map),
                      pl.BlockSpec((1, d, tn1), w_map),
                      pl.BlockSpec((1, 1, tn1), w_map),
                      pl.BlockSpec((PERM, PERM), lambda v, j, ve, vn, nu: (0, 0))],
            out_specs=pl.BlockSpec((r, tn1 // 2), o_map),
            scratch_shapes=[pltpu.VMEM((d, tn1), BF16)]),
        compiler_params=_cparams(("arbitrary", "arbitrary")),
        name="moe_up",
    )(vis_e, vis_n, n_used, xs, w1, b1p, perm)
    tn2 = _pick(d, 512)
    x_map, w_map, o_map = maps(d // tn2)
    return pl.pallas_call(
        _moe2_kernel,
        out_shape=jax.ShapeDtypeStruct(((n_vis + 1) * r, d), F32),
        grid_spec=pltpu.PrefetchScalarGridSpec(
            num_scalar_prefetch=3, grid=(n_vis, d // tn2),
            in_specs=[pl.BlockSpec((r, dh), x_map),
                      pl.BlockSpec((1, dh, tn2), w_map),
                      pl.BlockSpec((1, 1, tn2), w_map)],
            out_specs=pl.BlockSpec((r, tn2), o_map),
            scratch_shapes=[pltpu.VMEM((dh, tn2), BF16)]),
        compiler_params=_cparams(("arbitrary", "arbitrary")),
        name="moe_down",
    )(vis_e, vis_n, n_used, act, w2, b2.reshape(n_exp, 1, d))


def _perm_matrix():
    src = jnp.arange(PERM)
    dst = jnp.where(src % 2 == 0, src // 2, PERM // 2 + src // 2)
    return (dst[:, None] == jnp.arange(PERM)[None, :]).astype(BF16)


def _deinterleave_bias(b1):
    n_exp, dh2 = b1.shape
    b = b1.reshape(n_exp, dh2 // PERM, PERM // 2, 2)
    return jnp.concatenate([b[..., 0], b[..., 1]], axis=-1).reshape(n_exp, 1, dh2)


def _routing(top_e, n_tok, rows_per_visit):
    r = rows_per_visit
    n_assign = n_tok * TOP_K
    n_vis = n_assign // r + N_EXPERTS
    flat_e = top_e.reshape(-1)
    onehot = (flat_e[:, None] == jnp.arange(N_EXPERTS)[None, :]).astype(jnp.int32)
    rank = jnp.take_along_axis(jnp.cumsum(onehot, axis=0) - onehot, flat_e[:, None], axis=1)[:, 0]
    counts = jnp.sum(onehot, axis=0)
    seg_vis = (counts + r - 1) // r
    vis_end = jnp.cumsum(seg_vis)
    vis_start = vis_end - seg_vis
    dest = vis_start[flat_e] * r + rank
    v = jnp.arange(n_vis)
    n_used = vis_end[-1]
    vc = jnp.minimum(v, n_used - 1)
    vis_e = jnp.minimum(jnp.searchsorted(vis_end, vc, side="right"), N_EXPERTS - 1).astype(jnp.int32)
    valid = jnp.clip(counts[vis_e] - (vc - vis_start[vis_e]) * r, 0, r)
    vis_n = jnp.where(v < n_used, (valid + MOE_SUB - 1) // MOE_SUB, 0).astype(jnp.int32)
    return dest.astype(jnp.int32), vis_e, vis_n, n_used.reshape(1).astype(jnp.int32), n_vis


def _resid_kernel(x_ref, y_ref, mod_ref, fw_ref, o_ref, *, subs_per_batch, n_batch, final):
    m = mod_ref[_group_of(pl.program_id(0), subs_per_batch, n_batch)]
    x_new = x_ref[...] + m[5:6] * y_ref[...]
    o_ref[...] = _rms(x_new, NORM_EPS) * fw_ref[...] if final else x_new


def _resid(xt, y, mod, final_w, *, tm, subs_per_batch, n_batch, final):
    n, d = xt.shape
    kern = functools.partial(_resid_kernel, subs_per_batch=subs_per_batch, n_batch=n_batch, final=final)
    row = pl.BlockSpec((tm, d), lambda i: (i, 0))
    return pl.pallas_call(
        kern,
        out_shape=jax.ShapeDtypeStruct((n, d), F32),
        grid=(n // tm,),
        in_specs=[row, row, pl.BlockSpec((MOD_ROWS, 6, d), lambda i: (0, 0, 0)), pl.BlockSpec((1, d), lambda i: (0, 0))],
        out_specs=row,
        compiler_params=_cparams(("parallel",)),
        name="moe_resid",
    )(xt, y, mod, final_w.reshape(1, d))


def _pad_cols(w, cuts):
    parts = []
    for start, width, padded in cuts:
        parts.append(w[:, start:start + width])
        if padded > width:
            parts.append(jnp.zeros((w.shape[0], padded - width), w.dtype))
    return jnp.concatenate(parts, axis=1)


def kernel(x, c, ctx, c_ctx, w_mod, b_mod, norm1_w, w_in, gla_gate_w, gla_gate_b, gla_norm_w, gdn_conv_w,
           gdn_a_log, gdn_dt_bias, gdn_norm_w, diff_lambda, diff_norm_w, w_out, norm2_w, router_w, router_b,
           expert_w1, expert_b1, expert_w2, expert_b2, final_norm_w):
    n_batch, s_len, d = x.shape
    l_ctx = ctx.shape[1]
    depth = w_mod.shape[0]
    t_len = l_ctx + s_len
    n_tok = n_batch * t_len
    subs_per_batch = t_len // l_ctx
    assert s_len % l_ctx == 0 and l_ctx % CHUNK == 0 and n_batch < MOD_ROWS and l_ctx % 16 == 0
    rows_per_visit = 2048

    xt = jnp.concatenate([ctx, x], axis=1).reshape(n_tok, d)
    cvec = jnp.concatenate([c, c_ctx[None, :], jnp.zeros((MOD_ROWS - n_batch - 1, d), F32)], axis=0)
    mod_all = _modulation(cvec, w_mod, b_mod).reshape(depth, MOD_ROWS, 6, d)
    cos_t, sa_t, sb_t = _rope_tables(t_len, l_ctx)
    perm = _perm_matrix()
    a_w = GLA_HEADS * (2 * GLA_DK + 2 * GLA_DV)
    b_w = GDN_HEADS * (2 * GDN_DK + 2 * GDN_DV)
    c_w = DIFF_HEADS * (4 * DIFF_DH + DIFF_DV)
    glr0, b0 = a_w, a_w + 2 * GLA_RANK
    ba0 = b0 + b_w
    c0 = ba0 + 4 * GDN_HEADS
    cuts = ((0, a_w, a_w), (b0, b_w, b_w), (c0, c_w, c_w), (glr0, 2 * GLA_RANK, LANES), (ba0, 4 * GDN_HEADS, LANES))
    assert (a_w, a_w + b_w, a_w + b_w + c_w) == (B_Q, C_Q, A_GLR)

    for l in range(depth):
        lambda_init = 0.8 - 0.6 * math.exp(-0.3 * l)
        mod = mod_all[l]
        w_pad = _pad_cols(w_in[l], cuts).astype(BF16)
        p = _in_proj(xt, norm1_w[l], mod, w_pad, sub=l_ctx, subs_per_batch=subs_per_batch, n_batch=n_batch)

        gw = gla_gate_w[l]
        gw_pad = jnp.zeros((2, LANES, GLA_HEADS * GLA_DK), F32)
        gw_pad = gw_pad.at[0, 0:GLA_RANK].set(gw[0]).at[1, GLA_RANK:2 * GLA_RANK].set(gw[1])
        oa = _gla(p, gw_pad, gla_gate_b[l].reshape(2, 1, -1), n_batch=n_batch, t_len=t_len, l_ctx=l_ctx)

        ad = jnp.zeros((8, LANES), F32)
        ad = ad.at[0, 2 * GDN_HEADS:4 * GDN_HEADS].set(gdn_a_log[l].reshape(-1))
        ad = ad.at[1, 2 * GDN_HEADS:4 * GDN_HEADS].set(gdn_dt_bias[l].reshape(-1))
        ob = _gdn(p, gdn_conv_w[l], ad, n_batch=n_batch, t_len=t_len, l_ctx=l_ctx)

        qk = _rope(p, cos_t, sa_t, sb_t, tm=l_ctx)
        yd = _diff_attn(qk, p, diff_lambda[l], diff_norm_w[l], n_batch=n_batch, t_len=t_len, l_ctx=l_ctx,
                        lambda_init=lambda_init)

        rw_pad = jnp.zeros((d, LANES), F32).at[:, :N_EXPERTS].set(router_w[l])
        rb_pad = jnp.full((1, LANES), -1e30, F32).at[0, :N_EXPERTS].set(router_b[l])
        xt, h2, top_e, top_g = _out_proj(oa, ob, yd, p, gla_norm_w[l], gdn_norm_w[l], w_out[l].astype(BF16), xt, mod,
                                         norm2_w[l], rw_pad, rb_pad, tm=l_ctx, subs_per_batch=subs_per_batch,
                                         n_batch=n_batch)

        dest, vis_e, vis_n, n_used, n_vis = _routing(top_e[:, :TOP_K], n_tok, rows_per_visit)
        row_tok = jnp.zeros((n_vis * rows_per_visit,), jnp.int32).at[dest].set(
            jnp.arange(n_tok * TOP_K, dtype=jnp.int32) // TOP_K)
        xs = h2[row_tok]
        yb = _moe_experts(xs, vis_e, vis_n, n_used, expert_w1[l], _deinterleave_bias(expert_b1[l]),
                          expert_w2[l], expert_b2[l], perm, rows_per_visit=rows_per_visit)
        y = jnp.sum(yb[dest].reshape(n_tok, TOP_K, d) * top_g[:, :TOP_K, None], axis=1)
        xt = _resid(xt, y, mod, final_norm_w, tm=l_ctx, subs_per_batch=subs_per_batch, n_batch=n_batch,
                    final=(l == depth - 1))
    return xt.reshape(n_batch, t_len, d)[:, l_ctx:, :]
```

```python
import functools
import math

import jax
import jax.numpy as jnp
from jax import lax
from jax.experimental import pallas as pl
from jax.experimental.pallas import tpu as pltpu

F32 = jnp.float32
BF16 = jnp.bfloat16

GRID_W = 64
GLA_HEADS, GLA_DK, GLA_DV, GLA_RANK, GLA_GATE_NORM = 4, 64, 128, 16, 16.0
GDN_HEADS, GDN_DK, GDN_DV, GDN_CONV = 6, 128, 128, 5
DIFF_HEADS, DIFF_DH, DIFF_DV, DIFF_EPS = 6, 64, 128, 1e-5
ROPE_THETA = 10000.0
CHUNK = 64
N_EXPERTS, TOP_K = 32, 4
SWIGLU_ALPHA, SWIGLU_LIMIT = 1.702, 7.0
NORM_EPS = 1e-6
LANES = 128
MOD_ROWS = 8

A_Q, A_K, A_V, A_OG = 0, 256, 512, 1024
B_Q, B_K, B_V, B_OG = 1536, 2304, 3072, 3840
C_Q, C_K, C_V = 4608, 5376, 6144
A_GLR, B_BA = 6912, 7040
NP_IN = 7168
D_MIX = GLA_HEADS * GLA_DV + GDN_HEADS * GDN_DV + DIFF_HEADS * DIFF_DV

VMEM_LIMIT = 56 * 1024 * 1024

HI = lax.Precision.HIGHEST


def _dot(a, b):
    return jnp.dot(a, b, preferred_element_type=F32)


def _dot_hi(a, b):
    return jnp.dot(a, b, preferred_element_type=F32, precision=HI)


def _dot_nt(a, b, precision=None):
    return lax.dot_general(a, b, (((1,), (1,)), ((), ())), preferred_element_type=F32, precision=precision)


def _pick(n, pref):
    if n <= pref:
        return n
    t = pref - pref % LANES
    while t > LANES and n % t:
        t -= LANES
    return t


def _silu(x):
    return x * jax.nn.sigmoid(x)


def _log_sigmoid(x):
    return jnp.minimum(x, 0.0) - jnp.log(1.0 + jnp.exp(-jnp.abs(x)))


def _softplus(x):
    return jnp.maximum(x, 0.0) + jnp.log(1.0 + jnp.exp(-jnp.abs(x)))


def _cparams(sem):
    return pltpu.CompilerParams(dimension_semantics=sem, vmem_limit_bytes=VMEM_LIMIT)


def _mod_kernel(c_ref, w_ref, b_ref, o_ref):
    a = _silu(c_ref[...])
    o_ref[0] = _dot(a.astype(BF16), w_ref[0].astype(BF16)) + b_ref[0]


def _modulation(cvec, w_mod, b_mod):
    depth, d, n6 = w_mod.shape
    tn = _pick(n6, 1024)
    return pl.pallas_call(
        _mod_kernel,
        out_shape=jax.ShapeDtypeStruct((depth, MOD_ROWS, n6), F32),
        grid=(depth, n6 // tn),
        in_specs=[pl.BlockSpec((MOD_ROWS, d), lambda l, j: (0, 0)),
                  pl.BlockSpec((1, d, tn), lambda l, j: (l, 0, j)),
                  pl.BlockSpec((1, 1, tn), lambda l, j: (l, 0, j))],
        out_specs=pl.BlockSpec((1, MOD_ROWS, tn), lambda l, j: (l, 0, j)),
        compiler_params=_cparams(("parallel", "parallel")),
        name="modulation",
    )(cvec, w_mod, b_mod.reshape(depth, 1, n6))


def _group_of(sub_block, subs_per_batch, n_batch):
    return jnp.where(sub_block % subs_per_batch == 0, n_batch, sub_block // subs_per_batch)


def _rms(x, eps):
    return x * lax.rsqrt(jnp.mean(x * x, axis=-1, keepdims=True) + eps)


def _in_proj_kernel(x_ref, nw_ref, mod_ref, w_ref, o_ref, h_scr, *, sub, subs_per_batch, n_batch):
    i = pl.program_id(0)

    @pl.when(pl.program_id(1) == 0)
    def _():
        for s in range(x_ref.shape[0] // sub):
            m = mod_ref[_group_of(i * (x_ref.shape[0] // sub) + s, subs_per_batch, n_batch)]
            y = _rms(x_ref[s * sub:(s + 1) * sub, :], NORM_EPS) * nw_ref[...]
            h_scr[s * sub:(s + 1) * sub, :] = (y * (1.0 + m[1:2]) + m[0:1]).astype(BF16)

    o_ref[...] = _dot(h_scr[...], w_ref[...])


def _in_proj(xt, norm_w, mod, w_pad, *, sub, subs_per_batch, n_batch):
    n, d = xt.shape
    npad = w_pad.shape[1]
    tm = sub * 2 if (n // sub) % 2 == 0 else sub
    tn = _pick(npad, 512)
    kern = functools.partial(_in_proj_kernel, sub=sub, subs_per_batch=subs_per_batch, n_batch=n_batch)
    return pl.pallas_call(
        kern,
        out_shape=jax.ShapeDtypeStruct((n, npad), F32),
        grid=(n // tm, npad // tn),
        in_specs=[pl.BlockSpec((tm, d), lambda i, j: (i, 0)),
                  pl.BlockSpec((1, d), lambda i, j: (0, 0)),
                  pl.BlockSpec((MOD_ROWS, 6, d), lambda i, j: (0, 0, 0)),
                  pl.BlockSpec((d, tn), lambda i, j: (0, j))],
        out_specs=pl.BlockSpec((tm, tn), lambda i, j: (i, j)),
        scratch_shapes=[pltpu.VMEM((tm, d), BF16)],
        compiler_params=_cparams(("parallel", "arbitrary")),
        name="in_proj",
    )(xt, norm_w.reshape(1, d), mod, w_pad)


def _tri_masks(c):
    r = lax.broadcasted_iota(jnp.int32, (c, c), 0)
    s = lax.broadcasted_iota(jnp.int32, (c, c), 1)
    return ((s <= r, s < r), (s >= r, s > r))


def _bwd_chunk(i, nc_ctx, nc):
    return jnp.where(i < nc_ctx, nc_ctx - 1 - i, nc - 1 - (i - nc_ctx))


def _gla_kernel(q_ref, k_ref, v_ref, glr_ref, gw_ref, gb_ref, o_ref, st_scr, *, nc_ctx):
    t_len = q_ref.shape[0]
    nc = t_len // CHUNK
    masks = _tri_masks(CHUNK)
    lane = lax.broadcasted_iota(jnp.int32, (1, LANES), 1)
    head_mask = [(lane < GLA_DK).astype(F32), (lane >= GLA_DK).astype(F32)]
    o_ref[...] = jnp.zeros(o_ref.shape, F32)
    st_scr[...] = jnp.zeros(st_scr.shape, F32)

    def step(i, carry):
        cs = (i, _bwd_chunk(i, nc_ctx, nc))
        rows = [pl.ds(pl.multiple_of(c * CHUNK, CHUNK), CHUNK) for c in cs]
        pre = [_dot_hi(glr_ref[rows[z], :], gw_ref[z]) + gb_ref[z] for z in range(2)]
        b = [_dot_hi(masks[z][0].astype(F32), _log_sigmoid(pre[z]) / GLA_GATE_NORM) for z in range(2)]
        tot = [b[0][CHUNK - 1:CHUNK, :], b[1][0:1, :]]
        qd, kk, kd = [], [], []
        for z in range(2):
            k = k_ref[rows[z], :]
            qd.append(q_ref[rows[z], :] * GLA_DK ** -0.5 * jnp.exp(b[z]))
            kk.append((k * jnp.exp(-b[z])).astype(BF16))
            kd.append(k * jnp.exp(tot[z] - b[z]))
        zj = [(z, j) for z in range(2) for j in range(2)]
        qdj = [(qd[z] * head_mask[j]).astype(BF16) for z, j in zj]
        vj = [v_ref[rows[z], j * GLA_DV:(j + 1) * GLA_DV] for z, j in zj]
        st = [st_scr[z, j] for z, j in zj]
        att = [jnp.where(masks[z][0], _dot_nt(qdj[n], kk[z]), 0.0).astype(BF16) for n, (z, j) in enumerate(zj)]
        inter = [_dot_nt(qdj[n], st[n].astype(BF16)) for n in range(4)]
        intra = [_dot(att[n], vj[n].astype(BF16)) for n in range(4)]
        upd = [_dot(vj[n].T.astype(BF16), (kd[z] * head_mask[j]).astype(BF16)) for n, (z, j) in enumerate(zj)]
        for n, (z, j) in enumerate(zj):
            o_ref[rows[z], j * GLA_DV:(j + 1) * GLA_DV] += intra[n] + inter[n]
            st_scr[z, j] = st[n] * jnp.exp(tot[z]) + upd[n]
        return carry

    lax.fori_loop(0, nc, step, 0)


def _gla(p, gw_pad, gb, *, n_batch, t_len, l_ctx):
    n = p.shape[0]
    spec = lambda col, w: pl.BlockSpec((t_len, w), lambda b, hp, col=col, w=w: (b, col // w + hp))
    kern = functools.partial(_gla_kernel, nc_ctx=l_ctx // CHUNK)
    return pl.pallas_call(
        kern,
        out_shape=jax.ShapeDtypeStruct((n, GLA_HEADS * GLA_DV), F32),
        grid=(n_batch, GLA_HEADS // 2),
        in_specs=[spec(A_Q, LANES), spec(A_K, LANES), spec(A_V, 2 * GLA_DV),
                  pl.BlockSpec((t_len, LANES), lambda b, hp: (b, A_GLR // LANES)),
                  pl.BlockSpec((2, LANES, LANES), lambda b, hp: (0, 0, hp)),
                  pl.BlockSpec((2, 1, LANES), lambda b, hp: (0, 0, hp))],
        out_specs=pl.BlockSpec((t_len, 2 * GLA_DV), lambda b, hp: (b, hp)),
        scratch_shapes=[pltpu.VMEM((2, 2, GLA_DV, LANES), F32)],
        compiler_params=_cparams(("parallel", "parallel")),
        name="gla_scan",
    )(p, p, p, p, gw_pad, gb)


def _gdn_kernel(q_ref, k_ref, v_ref, ba_ref, cwq_ref, cwk_ref, cwv_ref, ad_ref, o_ref,
                u_scr, kd_scr, w_scr, qg_scr, a_scr, dec_scr, s_scr, *, nc_ctx, l_ctx, group):
    t_len = q_ref.shape[0]
    nc = t_len // CHUNK
    h = pl.program_id(1)
    masks = _tri_masks(CHUNK)
    lane = lax.broadcasted_iota(jnp.int32, (1, LANES), 1)
    halo = 8
    pad = (GDN_CONV - 1) // 2
    ones_l0 = jnp.broadcast_to((lane == 0).astype(F32), (CHUNK, LANES))

    def conv(src_ref, cw_ref, r0):
        top = src_ref[pl.ds(pl.multiple_of(jnp.maximum(r0 - halo, 0), halo), halo), :]
        bot = src_ref[pl.ds(pl.multiple_of(jnp.minimum(r0 + CHUNK, t_len - halo), halo), halo), :]
        win = jnp.concatenate([top, src_ref[pl.ds(r0, CHUNK), :], bot], axis=0)
        row = r0 + lax.broadcasted_iota(jnp.int32, (CHUNK, 1), 0)
        acc = jnp.zeros((CHUNK, LANES), F32)
        for j in range(GDN_CONV):
            d = j - pad
            ok = (((row + d) >= l_ctx) == (row >= l_ctx)) & (row + d >= 0) & (row + d < t_len)
            acc = acc + jnp.where(ok, win[halo + d:halo + d + CHUNK, :], 0.0) * cw_ref[j:j + 1, :]
        return _silu(acc)

    def l2n(y):
        return y * lax.rsqrt(jnp.sum(y * y, axis=-1, keepdims=True) + 1e-6)

    def prep(gi, carry):
        per_chunk, chains = [], []
        for cc in range(group):
            c = gi * group + cc
            r0 = pl.multiple_of(c * CHUNK, CHUNK)
            q = l2n(conv(q_ref, cwq_ref, r0)) * GDN_DK ** -0.5
            k = l2n(conv(k_ref, cwk_ref, r0))
            v = conv(v_ref, cwv_ref, r0)
            ba = ba_ref[pl.ds(r0, CHUNK), :]
            beta_all = jax.nn.sigmoid(ba)
            g_all = -jnp.exp(ad_ref[0:1, :]) * _softplus(ba + ad_ref[1:2, :])
            per_chunk.append((c, r0, q, k, v, beta_all, g_all))
        grams = []
        for (_, _, q, k, _, _, _) in per_chunk:
            kbf = k.astype(BF16)
            grams.append((_dot_nt(kbf, kbf), _dot_nt(q.astype(BF16), kbf)))
        gcs = []
        for (_, _, _, _, _, beta_all, g_all) in per_chunk:
            for z in range(2):
                col = z * GDN_HEADS + h
                beta = jnp.sum(jnp.where(lane == col, beta_all, 0.0), axis=-1, keepdims=True)
                g = jnp.sum(jnp.where(lane == 2 * GDN_HEADS + col, g_all, 0.0), axis=-1, keepdims=True)
                gcs.append((beta, _dot_hi(masks[z][0].astype(F32), jnp.broadcast_to(g, (CHUNK, LANES)))))
        grows = [_dot_nt(ones_l0, gc, precision=HI) for (_, gc) in gcs]
        xs, npows = [], []
        for ci, (c, r0, q, k, v, _, _) in enumerate(per_chunk):
            kk, qk = grams[ci]
            for z in range(2):
                incl, strict = masks[z]
                beta, gc = gcs[2 * ci + z]
                dmat = jnp.exp(jnp.where(incl, gc[:, :CHUNK] - grows[2 * ci + z], -jnp.inf))
                tot = gc[CHUNK - 1:CHUNK, :] if z == 0 else gc[0:1, :]
                rows = pl.ds(r0, CHUNK)
                a_scr[z, rows, :] = jnp.where(incl, qk * dmat, 0.0).astype(BF16)
                qg_scr[z, rows, :] = (q * jnp.exp(gc)).astype(BF16)
                kd_scr[z, rows, :] = k * jnp.exp(tot - gc)
                dec_scr[z, pl.ds(pl.multiple_of(c * halo, halo), halo), :] = jnp.broadcast_to(jnp.exp(tot), (halo, LANES))
                npows.append((-jnp.where(strict, beta * kk * dmat, 0.0)).astype(BF16))
                xs.append(jnp.concatenate([v * beta, k * (beta * jnp.exp(gc))], axis=1))
        for lvl in range(6):
            xs = [x + _dot(npw, x.astype(BF16)) for x, npw in zip(xs, npows)]
            if lvl < 5:
                npows = [_dot(npw, npw).astype(BF16) for npw in npows]
        for ci, (c, r0, _, _, _, _, _) in enumerate(per_chunk):
            for z in range(2):
                x = xs[2 * ci + z]
                u_scr[z, pl.ds(r0, CHUNK), :] = x[:, :GDN_DV]
                w_scr[z, pl.ds(r0, CHUNK), :] = x[:, GDN_DV:].astype(BF16)
        return carry

    lax.fori_loop(0, nc // group, prep, 0)

    o_ref[...] = jnp.zeros(o_ref.shape, F32)
    s_scr[...] = jnp.zeros(s_scr.shape, F32)

    def step(i, carry):
        cs = (i, _bwd_chunk(i, nc_ctx, nc))
        rows = [pl.ds(pl.multiple_of(c * CHUNK, CHUNK), CHUNK) for c in cs]
        s = [s_scr[z] for z in range(2)]
        sb = [s[z].astype(BF16) for z in range(2)]
        kdt = [kd_scr[z, rows[z], :].T.astype(BF16) for z in range(2)]
        ws = [_dot(w_scr[z, rows[z], :], sb[z]) for z in range(2)]
        qs = [_dot(qg_scr[z, rows[z], :], sb[z]) for z in range(2)]
        v_new = [(u_scr[z, rows[z], :] - ws[z]).astype(BF16) for z in range(2)]
        upd = [_dot(kdt[z], v_new[z]) for z in range(2)]
        av = [_dot(a_scr[z, rows[z], :], v_new[z]) for z in range(2)]
        for z in range(2):
            dec = dec_scr[z, pl.ds(pl.multiple_of(cs[z] * halo, halo), 1), :]
            s_scr[z] = s[z] * dec + upd[z]
            o_ref[rows[z], :] += qs[z] + av[z]
        return carry

    lax.fori_loop(0, nc, step, 0)


def _gdn(p, conv_w, ad, *, n_batch, t_len, l_ctx):
    n = p.shape[0]
    spec = lambda col: pl.BlockSpec((t_len, LANES), lambda b, h, col=col: (b, col // LANES + h))
    cspec = lambda off: pl.BlockSpec((GDN_CONV, LANES), lambda b, h, off=off: (0, off + h))
    nc = t_len // CHUNK
    group = next(g for g in (4, 2, 1) if nc % g == 0)
    kern = functools.partial(_gdn_kernel, nc_ctx=l_ctx // CHUNK, l_ctx=l_ctx, group=group)
    return pl.pallas_call(
        kern,
        out_shape=jax.ShapeDtypeStruct((n, GDN_HEADS * GDN_DV), F32),
        grid=(n_batch, GDN_HEADS),
        in_specs=[spec(B_Q), spec(B_K), spec(B_V),
                  pl.BlockSpec((t_len, LANES), lambda b, h: (b, B_BA // LANES)),
                  cspec(0), cspec(GDN_HEADS), cspec(2 * GDN_HEADS),
                  pl.BlockSpec((8, LANES), lambda b, h: (0, 0))],
        out_specs=pl.BlockSpec((t_len, GDN_DV), lambda b, h: (b, h)),
        scratch_shapes=[pltpu.VMEM((2, t_len, GDN_DV), F32),
                        pltpu.VMEM((2, t_len, GDN_DK), F32),
                        pltpu.VMEM((2, t_len, GDN_DK), BF16),
                        pltpu.VMEM((2, t_len, GDN_DK), BF16),
                        pltpu.VMEM((2, t_len, CHUNK), BF16),
                        pltpu.VMEM((2, nc * 8, LANES), F32),
                        pltpu.VMEM((2, GDN_DK, GDN_DV), F32)],
        compiler_params=_cparams(("parallel", "parallel")),
        name="gdn_scan",
    )(p, p, p, p, conv_w, conv_w, conv_w, ad)


def _rope_kernel(x_ref, cos_ref, sa_ref, sb_ref, o_ref):
    scale = jnp.where(pl.program_id(1) == 0, DIFF_DH ** -0.5, 1.0)
    cos, sa, sb = cos_ref[...] * scale, sa_ref[...] * scale, sb_ref[...] * scale
    for hh in range(DIFF_HEADS):
        x = x_ref[:, hh * LANES:(hh + 1) * LANES]
        y = x * cos + pltpu.roll(x, LANES - 16, 1) * sa + pltpu.roll(x, 16, 1) * sb
        o_ref[:, hh * LANES:(hh + 1) * LANES] = y.astype(BF16)


def _rope(p, cos_t, sa_t, sb_t, *, tm):
    n = p.shape[0]
    tblocks = cos_t.shape[0] // tm
    wq = DIFF_HEADS * LANES
    tspec = pl.BlockSpec((tm, LANES), lambda i, j: (i % tblocks, 0))
    return pl.pallas_call(
        _rope_kernel,
        out_shape=jax.ShapeDtypeStruct((n, 2 * wq), BF16),
        grid=(n // tm, 2),
        in_specs=[pl.BlockSpec((tm, wq), lambda i, j: (i, C_Q // wq + j)), tspec, tspec, tspec],
        out_specs=pl.BlockSpec((tm, wq), lambda i, j: (i, j)),
        compiler_params=_cparams(("parallel", "parallel")),
        name="rope",
    )(p, cos_t, sa_t, sb_t)


def _rope_tables(t_len, l_ctx):
    pos = jnp.arange(t_len - l_ctx)
    row = (pos // GRID_W).astype(F32)
    colp = (pos % GRID_W).astype(F32)
    half = DIFF_DH // 2
    inv = 1.0 / (ROPE_THETA ** (jnp.arange(0, half, 2, dtype=F32) / half))
    lane = jnp.arange(LANES)
    part = (lane % DIFF_DH) // half
    upper = (lane % half) // (half // 2)
    ang = jnp.where(part[None, :] == 0, row[:, None], colp[:, None]) * inv[lane % (half // 2)][None, :]
    cos, sin = jnp.cos(ang), jnp.sin(ang)
    sa = jnp.where(upper[None, :] == 0, -sin, 0.0)
    sb = jnp.where(upper[None, :] == 1, sin, 0.0)
    ident = lambda v, fill: jnp.concatenate([jnp.full((l_ctx, LANES), fill, F32), v.astype(F32)], axis=0)
    return ident(cos, 1.0), ident(sa, 0.0), ident(sb, 0.0)


def _attn_kernel(q_ref, k_ref, v_ref, lam_ref, nw_ref, o_ref, *, l_ctx, lambda_init):
    tq = q_ref.shape[0]
    t_len = k_ref.shape[0]
    lv = lam_ref[...]
    lam = (jnp.exp(jnp.sum(lv[0:1] * lv[1:2], axis=-1, keepdims=True))
           - jnp.exp(jnp.sum(lv[2:3] * lv[3:4], axis=-1, keepdims=True)) + lambda_init)
    lane = lax.broadcasted_iota(jnp.int32, (1, LANES), 1)

    def attend(kv_len):
        q = q_ref[...]
        k = k_ref[0:kv_len, :]
        v = v_ref[0:kv_len, :].astype(BF16)
        probs = []
        for m in range(2):
            qm = jnp.where((lane >= DIFF_DH) == (m == 1), q, jnp.zeros_like(q))
            s = _dot_nt(qm, k)
            e = jnp.exp(s - jnp.max(s, axis=-1, keepdims=True))
            probs.append(e * (1.0 / jnp.sum(e, axis=-1, keepdims=True)))
        w = probs[0] - lam * probs[1]
        o = _dot(w.astype(BF16), v)
        o_ref[...] = _rms(o, DIFF_EPS) * nw_ref[...] * (1.0 - lambda_init)

    is_ctx = pl.program_id(2) < l_ctx // tq
    pl.when(is_ctx)(lambda: attend(l_ctx))
    pl.when(jnp.logical_not(is_ctx))(lambda: attend(t_len))


def _diff_attn(qk, p, lam_vecs, norm_w, *, n_batch, t_len, l_ctx, lambda_init):
    n = p.shape[0]
    tq = min(l_ctx, 256)
    nt = t_len // tq
    kern = functools.partial(_attn_kernel, l_ctx=l_ctx, lambda_init=lambda_init)
    return pl.pallas_call(
        kern,
        out_shape=jax.ShapeDtypeStruct((n, DIFF_HEADS * DIFF_DV), F32),
        grid=(n_batch, DIFF_HEADS, nt),
        in_specs=[pl.BlockSpec((tq, LANES), lambda b, h, t: (b * nt + t, h)),
                  pl.BlockSpec((t_len, LANES), lambda b, h, t: (b, DIFF_HEADS + h)),
                  pl.BlockSpec((t_len, LANES), lambda b, h, t: (b, C_V // LANES + h)),
                  pl.BlockSpec((4, DIFF_DH), lambda b, h, t: (0, 0)),
                  pl.BlockSpec((1, DIFF_DV), lambda b, h, t: (0, 0))],
        out_specs=pl.BlockSpec((tq, DIFF_DV), lambda b, h, t: (b * nt + t, h)),
        compiler_params=_cparams(("parallel", "parallel", "parallel")),
        name="diff_attn",
    )(qk, qk, p, lam_vecs, norm_w.reshape(1, DIFF_DV))


def _out_proj_kernel(oa_ref, oga_ref, ob_ref, ogb_ref, yd_ref, anw_ref, bnw_ref, w_ref, x_ref, mod_ref,
                     n2_ref, rw_ref, rb_ref, xo_ref, h2_ref, te_ref, tg_ref, *, subs_per_batch, n_batch):
    m = mod_ref[_group_of(pl.program_id(0), subs_per_batch, n_batch)]
    pieces = []
    for o_ref, og_ref, nw_ref, heads in ((oa_ref, oga_ref, anw_ref, GLA_HEADS), (ob_ref, ogb_ref, bnw_ref, GDN_HEADS)):
        for hh in range(heads):
            sl = slice(hh * LANES, (hh + 1) * LANES)
            pieces.append((_rms(o_ref[:, sl], NORM_EPS) * nw_ref[...] * _silu(og_ref[:, sl])).astype(BF16))
    pieces.append(yd_ref[...].astype(BF16))
    mix = jnp.concatenate(pieces, axis=1)
    x_new = x_ref[...] + m[2:3] * _dot(mix, w_ref[...])
    xo_ref[...] = x_new
    h2 = _rms(x_new, NORM_EPS) * n2_ref[...] * (1.0 + m[4:5]) + m[3:4]
    h2_ref[...] = h2.astype(BF16)
    lg = _dot_hi(h2, rw_ref[...]) + rb_ref[...]
    lane = lax.broadcasted_iota(jnp.int32, lg.shape, 1)
    lane_f = lane.astype(F32)
    vals, idxs = [], []
    for _ in range(TOP_K):
        mx = jnp.max(lg, axis=-1, keepdims=True)
        ix = jnp.min(jnp.where(lg == mx, lane_f, float(LANES)), axis=-1, keepdims=True)
        vals.append(mx)
        idxs.append(ix)
        lg = jnp.where(lane_f == ix, -jnp.inf, lg)
    ex = [jnp.exp(v - vals[0]) for v in vals]
    inv_den = 1.0 / (ex[0] + ex[1] + ex[2] + ex[3])
    te = jnp.zeros(lg.shape, F32)
    tg = jnp.zeros(lg.shape, F32)
    for kk in range(TOP_K):
        te = jnp.where(lane == kk, idxs[kk], te)
        tg = jnp.where(lane == kk, ex[kk] * inv_den, tg)
    te_ref[...] = te.astype(jnp.int32)
    tg_ref[...] = tg


def _out_proj(oa, ob, yd, p, gla_nw, gdn_nw, w_out, xt, mod, norm2_w, rw_pad, rb_pad, *, tm, subs_per_batch, n_batch):
    n, d = xt.shape
    kern = functools.partial(_out_proj_kernel, subs_per_batch=subs_per_batch, n_batch=n_batch)
    const = lambda shape: pl.BlockSpec(shape, lambda i: (0,) * len(shape))
    wa, wb, wd = GLA_HEADS * GLA_DV, GDN_HEADS * GDN_DV, DIFF_HEADS * DIFF_DV
    return pl.pallas_call(
        kern,
        out_shape=(jax.ShapeDtypeStruct((n, d), F32), jax.ShapeDtypeStruct((n, d), BF16),
                   jax.ShapeDtypeStruct((n, LANES), jnp.int32), jax.ShapeDtypeStruct((n, LANES), F32)),
        grid=(n // tm,),
        in_specs=[pl.BlockSpec((tm, wa), lambda i: (i, 0)),
                  pl.BlockSpec((tm, wa), lambda i: (i, A_OG // wa)),
                  pl.BlockSpec((tm, wb), lambda i: (i, 0)),
                  pl.BlockSpec((tm, wb), lambda i: (i, B_OG // wb)),
                  pl.BlockSpec((tm, wd), lambda i: (i, 0)),
                  const((1, LANES)), const((1, LANES)), const((D_MIX, d)),
                  pl.BlockSpec((tm, d), lambda i: (i, 0)),
                  const((MOD_ROWS, 6, d)), const((1, d)), const((d, LANES)), const((1, LANES))],
        out_specs=(pl.BlockSpec((tm, d), lambda i: (i, 0)), pl.BlockSpec((tm, d), lambda i: (i, 0)),
                   pl.BlockSpec((tm, LANES), lambda i: (i, 0)), pl.BlockSpec((tm, LANES), lambda i: (i, 0))),
        compiler_params=_cparams(("parallel",)),
        name="out_proj",
    )(oa, p, ob, p, yd, gla_nw.reshape(1, LANES), gdn_nw.reshape(1, LANES), w_out, xt, mod,
      norm2_w.reshape(1, d), rw_pad, rb_pad)


MOE_SUB = 256
PERM = 256


def _moe1_kernel(ve_ref, vn_ref, nu_ref, x_ref, w_ref, b_ref, perm_ref, o_ref, wp_scr):
    v = pl.program_id(0)
    tn = w_ref.shape[3]

    @pl.when(vn_ref[v] > 0)
    def _():
        for c in range(tn // PERM):
            wc = w_ref[0, 0, :, c * PERM:(c + 1) * PERM].astype(BF16)
            wp_scr[:, c * PERM:(c + 1) * PERM] = _dot(wc, perm_ref[...]).astype(BF16)

    def body(s, carry):
        rows = pl.ds(pl.multiple_of(s * MOE_SUB, MOE_SUB), MOE_SUB)
        hid = _dot(x_ref[rows, :], wp_scr[...]) + b_ref[0]
        for c in range(tn // PERM):
            glu = jnp.minimum(hid[:, c * PERM:c * PERM + PERM // 2], SWIGLU_LIMIT)
            lin = jnp.clip(hid[:, c * PERM + PERM // 2:(c + 1) * PERM], -SWIGLU_LIMIT, SWIGLU_LIMIT)
            act = glu * jax.nn.sigmoid(SWIGLU_ALPHA * glu) * (lin + 1.0)
            o_ref[rows, c * (PERM // 2):(c + 1) * (PERM // 2)] = act.astype(BF16)
        return carry

    lax.fori_loop(0, vn_ref[v], body, 0)


def _moe2_kernel(ve_ref, vn_ref, nu_ref, a_ref, w_ref, b_ref, o_ref, wb_scr):
    v = pl.program_id(0)

    @pl.when(vn_ref[v] > 0)
    def _():
        wb_scr[...] = w_ref[0, 0].astype(BF16)

    def body(s, carry):
        rows = pl.ds(pl.multiple_of(s * MOE_SUB, MOE_SUB), MOE_SUB)
        o_ref[rows, :] = _dot(a_ref[rows, :], wb_scr[...]) + b_ref[0]
        return carry

    lax.fori_loop(0, vn_ref[v], body, 0)


def _moe_experts(xs, vis_e, vis_n, n_used, w1, b1p, w2, b2, perm, *, layer, rows_per_visit):
    _, n_exp, d, dh2 = w1.shape
    dh = dh2 // 2
    r = rows_per_visit
    n_vis = xs.shape[0] // r

    def maps(n_j):
        live = lambda v, nu: v < nu[0]
        jj = lambda v, j, nu: jnp.where(live(v, nu), j, n_j - 1)
        x_map = lambda v, j, ve, vn, nu: (jnp.minimum(v, nu[0] - 1), 0)
        w_map = lambda v, j, ve, vn, nu: (layer, ve[v], 0, jj(v, j, nu))
        b_map = lambda v, j, ve, vn, nu: (ve[v], 0, jj(v, j, nu))
        o_map = lambda v, j, ve, vn, nu: (jnp.where(live(v, nu), v, n_vis), jj(v, j, nu))
        return x_map, w_map, b_map, o_map

    tn1 = _pick(dh2, 512)
    x_map, w_map, b_map, o_map = maps(dh2 // tn1)
    act = pl.pallas_call(
        _moe1_kernel,
        out_shape=jax.ShapeDtypeStruct(((n_vis + 1) * r, dh), BF16),
        grid_spec=pltpu.PrefetchScalarGridSpec(
            num_scalar_prefetch=3, grid=(n_vis, dh2 // tn1),
            in_specs=[pl.BlockSpec((r, d), x_map),
                      pl.BlockSpec((1, 1, d, tn1), w_map),
                      pl.BlockSpec((1, 1, tn1), b_map),
                      pl.BlockSpec((PERM, PERM), lambda v, j, ve, vn, nu: (0, 0))],
            out_specs=pl.BlockSpec((r, tn1 // 2), o_map),
            scratch_shapes=[pltpu.VMEM((d, tn1), BF16)]),
        compiler_params=_cparams(("arbitrary", "arbitrary")),
        name="moe_up",
    )(vis_e, vis_n, n_used, xs, w1, b1p, perm)
    tn2 = _pick(d, 512)
    x_map, w_map, b_map, o_map = maps(d // tn2)
    return pl.pallas_call(
        _moe2_kernel,
        out_shape=jax.ShapeDtypeStruct(((n_vis + 1) * r, d), F32),
        grid_spec=pltpu.PrefetchScalarGridSpec(
            num_scalar_prefetch=3, grid=(n_vis, d // tn2),
            in_specs=[pl.BlockSpec((r, dh), x_map),
                      pl.BlockSpec((1, 1, dh, tn2), w_map),
                      pl.BlockSpec((1, 1, tn2), b_map)],
            out_specs=pl.BlockSpec((r, tn2), o_map),
            scratch_shapes=[pltpu.VMEM((dh, tn2), BF16)]),
        compiler_params=_cparams(("arbitrary", "arbitrary")),
        name="moe_down",
    )(vis_e, vis_n, n_used, act, w2, b2.reshape(n_exp, 1, d))


def _perm_matrix():
    src = jnp.arange(PERM)
    dst = jnp.where(src % 2 == 0, src // 2, PERM // 2 + src // 2)
    return (dst[:, None] == jnp.arange(PERM)[None, :]).astype(BF16)


def _deinterleave_bias(b1):
    n_exp, dh2 = b1.shape
    b = b1.reshape(n_exp, dh2 // PERM, PERM // 2, 2)
    return jnp.concatenate([b[..., 0], b[..., 1]], axis=-1).reshape(n_exp, 1, dh2)


def _routing(top_e, n_tok, rows_per_visit):
    r = rows_per_visit
    n_assign = n_tok * TOP_K
    n_vis = n_assign // r + N_EXPERTS
    flat_e = top_e.reshape(-1)
    onehot = (flat_e[:, None] == jnp.arange(N_EXPERTS)[None, :]).astype(jnp.int32)
    rank = jnp.take_along_axis(jnp.cumsum(onehot, axis=0) - onehot, flat_e[:, None], axis=1)[:, 0]
    counts = jnp.sum(onehot, axis=0)
    seg_vis = (counts + r - 1) // r
    vis_end = jnp.cumsum(seg_vis)
    vis_start = vis_end - seg_vis
    dest = vis_start[flat_e] * r + rank
    v = jnp.arange(n_vis)
    n_used = vis_end[-1]
    vc = jnp.minimum(v, n_used - 1)
    vis_e = jnp.minimum(jnp.searchsorted(vis_end, vc, side="right"), N_EXPERTS - 1).astype(jnp.int32)
    valid = jnp.clip(counts[vis_e] - (vc - vis_start[vis_e]) * r, 0, r)
    vis_n = jnp.where(v < n_used, (valid + MOE_SUB - 1) // MOE_SUB, 0).astype(jnp.int32)
    return dest.astype(jnp.int32), vis_e, vis_n, n_used.reshape(1).astype(jnp.int32), n_vis


def _combine_kernel(idx_ref, x_ref, g_ref, mod_ref, fw_ref, yb_ref, o_ref, buf, sem, *,
                    sub, subs_per_batch, n_batch, final):
    tm = x_ref.shape[0]

    def row_copy(t, k, row):
        return pltpu.make_async_copy(yb_ref.at[pl.ds(row, 1), :], buf.at[k, pl.ds(t, 1), :], sem)

    def issue(t, carry):
        for k in range(TOP_K):
            row_copy(t, k, idx_ref[0, 0, t * TOP_K + k]).start()
        return carry

    def drain(t, carry):
        for k in range(TOP_K):
            row_copy(t, k, 0).wait()
        return carry

    lax.fori_loop(0, tm, issue, 0)
    lax.fori_loop(0, tm, drain, 0)
    g = g_ref[...]
    y = g[:, 0:1] * buf[0]
    for k in range(1, TOP_K):
        y = y + g[:, k:k + 1] * buf[k]
    m = mod_ref[_group_of((pl.program_id(0) * tm) // sub, subs_per_batch, n_batch)]
    x_new = x_ref[...] + m[5:6] * y
    o_ref[...] = _rms(x_new, NORM_EPS) * fw_ref[...] if final else x_new


def _combine(xt, yb, dest, top_g, mod, final_w, *, sub, subs_per_batch, n_batch, final):
    n, d = xt.shape
    tm = min(sub, 128)
    kern = functools.partial(_combine_kernel, sub=sub, subs_per_batch=subs_per_batch, n_batch=n_batch, final=final)
    row = pl.BlockSpec((tm, d), lambda i: (i, 0))
    return pl.pallas_call(
        kern,
        out_shape=jax.ShapeDtypeStruct((n, d), F32),
        grid=(n // tm,),
        in_specs=[pl.BlockSpec((1, 1, tm * TOP_K), lambda i: (i, 0, 0), memory_space=pltpu.SMEM),
                  row, pl.BlockSpec((tm, LANES), lambda i: (i, 0)),
                  pl.BlockSpec((MOD_ROWS, 6, d), lambda i: (0, 0, 0)), pl.BlockSpec((1, d), lambda i: (0, 0)),
                  pl.BlockSpec(memory_space=pl.ANY)],
        out_specs=row,
        scratch_shapes=[pltpu.VMEM((TOP_K, tm, d), F32), pltpu.SemaphoreType.DMA(())],
        compiler_params=_cparams(("arbitrary",)),
        name="moe_combine",
    )(dest.reshape(n // tm, 1, tm * TOP_K), xt, top_g, mod, final_w.reshape(1, d), yb)


def _pad_cols(w, cuts):
    parts = []
    for start, width, padded in cuts:
        parts.append(w[:, start:start + width])
        if padded > width:
            parts.append(jnp.zeros((w.shape[0], padded - width), w.dtype))
    return jnp.concatenate(parts, axis=1)


def kernel(x, c, ctx, c_ctx, w_mod, b_mod, norm1_w, w_in, gla_gate_w, gla_gate_b, gla_norm_w, gdn_conv_w,
           gdn_a_log, gdn_dt_bias, gdn_norm_w, diff_lambda, diff_norm_w, w_out, norm2_w, router_w, router_b,
           expert_w1, expert_b1, expert_w2, expert_b2, final_norm_w):
    n_batch, s_len, d = x.shape
    l_ctx = ctx.shape[1]
    depth = w_mod.shape[0]
    t_len = l_ctx + s_len
    n_tok = n_batch * t_len
    subs_per_batch = t_len // l_ctx
    assert s_len % l_ctx == 0 and l_ctx % CHUNK == 0 and n_batch < MOD_ROWS and l_ctx % 16 == 0
    rows_per_visit = 2048

    xt = jnp.concatenate([ctx, x], axis=1).reshape(n_tok, d)
    cvec = jnp.concatenate([c, c_ctx[None, :], jnp.zeros((MOD_ROWS - n_batch - 1, d), F32)], axis=0)
    mod_all = _modulation(cvec, w_mod, b_mod).reshape(depth, MOD_ROWS, 6, d)
    cos_t, sa_t, sb_t = _rope_tables(t_len, l_ctx)
    perm = _perm_matrix()
    a_w = GLA_HEADS * (2 * GLA_DK + 2 * GLA_DV)
    b_w = GDN_HEADS * (2 * GDN_DK + 2 * GDN_DV)
    c_w = DIFF_HEADS * (4 * DIFF_DH + DIFF_DV)
    glr0, b0 = a_w, a_w + 2 * GLA_RANK
    ba0 = b0 + b_w
    c0 = ba0 + 4 * GDN_HEADS
    cuts = ((0, a_w, a_w), (b0, b_w, b_w), (c0, c_w, c_w), (glr0, 2 * GLA_RANK, LANES), (ba0, 4 * GDN_HEADS, LANES))
    assert (a_w, a_w + b_w, a_w + b_w + c_w) == (B_Q, C_Q, A_GLR)

    for l in range(depth):
        lambda_init = 0.8 - 0.6 * math.exp(-0.3 * l)
        mod = mod_all[l]
        w_pad = _pad_cols(w_in[l], cuts).astype(BF16)
        p = _in_proj(xt, norm1_w[l], mod, w_pad, sub=l_ctx, subs_per_batch=subs_per_batch, n_batch=n_batch)

        gw = gla_gate_w[l]
        gw_pad = jnp.zeros((2, LANES, GLA_HEADS * GLA_DK), F32)
        gw_pad = gw_pad.at[0, 0:GLA_RANK].set(gw[0]).at[1, GLA_RANK:2 * GLA_RANK].set(gw[1])
        oa = _gla(p, gw_pad, gla_gate_b[l].reshape(2, 1, -1), n_batch=n_batch, t_len=t_len, l_ctx=l_ctx)

        ad = jnp.zeros((8, LANES), F32)
        ad = ad.at[0, 2 * GDN_HEADS:4 * GDN_HEADS].set(gdn_a_log[l].reshape(-1))
        ad = ad.at[1, 2 * GDN_HEADS:4 * GDN_HEADS].set(gdn_dt_bias[l].reshape(-1))
        ob = _gdn(p, gdn_conv_w[l], ad, n_batch=n_batch, t_len=t_len, l_ctx=l_ctx)

        qk = _rope(p, cos_t, sa_t, sb_t, tm=l_ctx)
        yd = _diff_attn(qk, p, diff_lambda[l], diff_norm_w[l], n_batch=n_batch, t_len=t_len, l_ctx=l_ctx,
                        lambda_init=lambda_init)

        rw_pad = jnp.zeros((d, LANES), F32).at[:, :N_EXPERTS].set(router_w[l])
        rb_pad = jnp.full((1, LANES), -1e30, F32).at[0, :N_EXPERTS].set(router_b[l])
        xt, h2, top_e, top_g = _out_proj(oa, ob, yd, p, gla_norm_w[l], gdn_norm_w[l], w_out[l].astype(BF16), xt, mod,
                                         norm2_w[l], rw_pad, rb_pad, tm=l_ctx, subs_per_batch=subs_per_batch,
                                         n_batch=n_batch)

        dest, vis_e, vis_n, n_used, n_vis = _routing(top_e[:, :TOP_K], n_tok, rows_per_visit)
        row_tok = jnp.zeros((n_vis * rows_per_visit,), jnp.int32).at[dest].set(
            jnp.arange(n_tok * TOP_K, dtype=jnp.int32) // TOP_K)
        xs = h2[row_tok]
        yb = _moe_experts(xs, vis_e, vis_n, n_used, expert_w1, _deinterleave_bias(expert_b1[l]),
                          expert_w2, expert_b2[l], perm, layer=l, rows_per_visit=rows_per_visit)
        xt = _combine(xt, yb, dest, top_g, mod, final_norm_w, sub=l_ctx, subs_per_batch=subs_per_batch,
                      n_batch=n_batch, final=(l == depth - 1))
    return xt.reshape(n_batch, t_len, d)[:, l_ctx:, :]
```

```python
import functools
import math

import jax
import jax.numpy as jnp
from jax import lax
from jax.experimental import pallas as pl
from jax.experimental.pallas import tpu as pltpu

F32 = jnp.float32
BF16 = jnp.bfloat16

GRID_W = 64
GLA_HEADS, GLA_DK, GLA_DV, GLA_RANK, GLA_GATE_NORM = 4, 64, 128, 16, 16.0
GDN_HEADS, GDN_DK, GDN_DV, GDN_CONV = 6, 128, 128, 5
DIFF_HEADS, DIFF_DH, DIFF_DV, DIFF_EPS = 6, 64, 128, 1e-5
ROPE_THETA = 10000.0
CHUNK = 64
N_EXPERTS, TOP_K = 32, 4
SWIGLU_ALPHA, SWIGLU_LIMIT = 1.702, 7.0
NORM_EPS = 1e-6
LANES = 128
MOD_ROWS = 8

A_Q, A_K, A_V, A_OG = 0, 256, 512, 1024
B_Q, B_K, B_V, B_OG = 1536, 2304, 3072, 3840
C_Q, C_K, C_V = 4608, 5376, 6144
A_GLR, B_BA = 6912, 7040
NP_IN = 7168
D_MIX = GLA_HEADS * GLA_DV + GDN_HEADS * GDN_DV + DIFF_HEADS * DIFF_DV

VMEM_LIMIT = 56 * 1024 * 1024

def _dot(a, b):
    return jnp.dot(a, b, preferred_element_type=F32)


def _dot_nt(a, b):
    return lax.dot_general(a, b, (((1,), (1,)), ((), ())), preferred_element_type=F32)


def _split3(x):
    hi = x.astype(BF16)
    r = x - hi.astype(F32)
    mid = r.astype(BF16)
    return hi, mid, (r - mid.astype(F32)).astype(BF16)


def _dot_sel(sel, x, nt=False):
    dot = _dot_nt if nt else _dot
    s = sel.astype(F32).astype(BF16)
    hi, mid, lo = _split3(x)
    return dot(s, hi) + dot(s, mid) + dot(s, lo)


def _dot3(a, b):
    a_hi = a.astype(BF16)
    b_hi = b.astype(BF16)
    a_lo = (a - a_hi.astype(F32)).astype(BF16)
    b_lo = (b - b_hi.astype(F32)).astype(BF16)
    return _dot(a_hi, b_hi) + _dot(a_hi, b_lo) + _dot(a_lo, b_hi)


def _pick(n, pref):
    if n <= pref:
        return n
    t = pref - pref % LANES
    while t > LANES and n % t:
        t -= LANES
    return t


def _silu(x):
    return x * jax.nn.sigmoid(x)


def _log_sigmoid(x):
    return jnp.minimum(x, 0.0) - jnp.log(1.0 + jnp.exp(-jnp.abs(x)))


def _softplus(x):
    return jnp.maximum(x, 0.0) + jnp.log(1.0 + jnp.exp(-jnp.abs(x)))


def _cparams(sem):
    return pltpu.CompilerParams(dimension_semantics=sem, vmem_limit_bytes=VMEM_LIMIT)


def _mod_kernel(c_ref, w_ref, b_ref, o_ref):
    a = _silu(c_ref[...])
    o_ref[0] = _dot(a.astype(BF16), w_ref[0].astype(BF16)) + b_ref[0]


def _modulation(cvec, w_mod, b_mod):
    depth, d, n6 = w_mod.shape
    tn = _pick(n6, 1024)
    return pl.pallas_call(
        _mod_kernel,
        out_shape=jax.ShapeDtypeStruct((depth, MOD_ROWS, n6), F32),
        grid=(depth, n6 // tn),
        in_specs=[pl.BlockSpec((MOD_ROWS, d), lambda l, j: (0, 0)),
                  pl.BlockSpec((1, d, tn), lambda l, j: (l, 0, j)),
                  pl.BlockSpec((1, 1, tn), lambda l, j: (l, 0, j))],
        out_specs=pl.BlockSpec((1, MOD_ROWS, tn), lambda l, j: (l, 0, j)),
        compiler_params=_cparams(("parallel", "parallel")),
        name="modulation",
    )(cvec, w_mod, b_mod.reshape(depth, 1, n6))


def _group_of(sub_block, subs_per_batch, n_batch):
    return jnp.where(sub_block % subs_per_batch == 0, n_batch, sub_block // subs_per_batch)


def _rms(x, eps):
    return x * lax.rsqrt(jnp.mean(x * x, axis=-1, keepdims=True) + eps)


def _in_proj_kernel(x_ref, nw_ref, mod_ref, w_ref, o_ref, h_scr, *, sub, subs_per_batch, n_batch):
    i = pl.program_id(0)

    @pl.when(pl.program_id(1) == 0)
    def _():
        for s in range(x_ref.shape[0] // sub):
            m = mod_ref[_group_of(i * (x_ref.shape[0] // sub) + s, subs_per_batch, n_batch)]
            y = _rms(x_ref[s * sub:(s + 1) * sub, :], NORM_EPS) * nw_ref[...]
            h_scr[s * sub:(s + 1) * sub, :] = (y * (1.0 + m[1:2]) + m[0:1]).astype(BF16)

    o_ref[...] = _dot(h_scr[...], w_ref[...])


def _in_proj(xt, norm_w, mod, w_pad, *, sub, subs_per_batch, n_batch):
    n, d = xt.shape
    npad = w_pad.shape[1]
    tm = sub * 2 if (n // sub) % 2 == 0 else sub
    tn = _pick(npad, 1024)
    kern = functools.partial(_in_proj_kernel, sub=sub, subs_per_batch=subs_per_batch, n_batch=n_batch)
    return pl.pallas_call(
        kern,
        out_shape=jax.ShapeDtypeStruct((n, npad), F32),
        grid=(n // tm, npad // tn),
        in_specs=[pl.BlockSpec((tm, d), lambda i, j: (i, 0)),
                  pl.BlockSpec((1, d), lambda i, j: (0, 0)),
                  pl.BlockSpec((MOD_ROWS, 6, d), lambda i, j: (0, 0, 0)),
                  pl.BlockSpec((d, tn), lambda i, j: (0, j))],
        out_specs=pl.BlockSpec((tm, tn), lambda i, j: (i, j)),
        scratch_shapes=[pltpu.VMEM((tm, d), BF16)],
        compiler_params=_cparams(("parallel", "arbitrary")),
        name="in_proj",
    )(xt, norm_w.reshape(1, d), mod, w_pad)


def _tri_masks(c):
    r = lax.broadcasted_iota(jnp.int32, (c, c), 0)
    s = lax.broadcasted_iota(jnp.int32, (c, c), 1)
    return ((s <= r, s < r), (s >= r, s > r))


def _bwd_chunk(i, nc_ctx, nc):
    return jnp.where(i < nc_ctx, nc_ctx - 1 - i, nc - 1 - (i - nc_ctx))


def _gla_kernel(q_ref, k_ref, v_ref, glr_ref, gw_ref, gb_ref, o_ref, st_scr, *, nc_ctx):
    t_len = q_ref.shape[0]
    nc = t_len // CHUNK
    masks = _tri_masks(CHUNK)
    lane = lax.broadcasted_iota(jnp.int32, (1, LANES), 1)
    head_mask = [(lane < GLA_DK).astype(F32), (lane >= GLA_DK).astype(F32)]
    o_ref[...] = jnp.zeros(o_ref.shape, F32)
    st_scr[...] = jnp.zeros(st_scr.shape, F32)

    def step(i, carry):
        cs = (i, _bwd_chunk(i, nc_ctx, nc))
        rows = [pl.ds(pl.multiple_of(c * CHUNK, CHUNK), CHUNK) for c in cs]
        pre = [_dot3(glr_ref[rows[z], :], gw_ref[z]) + gb_ref[z] for z in range(2)]
        b = [_dot_sel(masks[z][0], _log_sigmoid(pre[z]) / GLA_GATE_NORM) for z in range(2)]
        tot = [b[0][CHUNK - 1:CHUNK, :], b[1][0:1, :]]
        qd, kk, kd = [], [], []
        for z in range(2):
            k = k_ref[rows[z], :]
            qd.append(q_ref[rows[z], :] * GLA_DK ** -0.5 * jnp.exp(b[z]))
            kk.append((k * jnp.exp(-b[z])).astype(BF16))
            kd.append(k * jnp.exp(tot[z] - b[z]))
        zj = [(z, j) for z in range(2) for j in range(2)]
        qdj = [(qd[z] * head_mask[j]).astype(BF16) for z, j in zj]
        vj = [v_ref[rows[z], j * GLA_DV:(j + 1) * GLA_DV] for z, j in zj]
        st = [st_scr[z, j] for z, j in zj]
        att = [jnp.where(masks[z][0], _dot_nt(qdj[n], kk[z]), 0.0).astype(BF16) for n, (z, j) in enumerate(zj)]
        inter = [_dot_nt(qdj[n], st[n].astype(BF16)) for n in range(4)]
        intra = [_dot(att[n], vj[n].astype(BF16)) for n in range(4)]
        upd = [_dot(vj[n].T.astype(BF16), (kd[z] * head_mask[j]).astype(BF16)) for n, (z, j) in enumerate(zj)]
        for n, (z, j) in enumerate(zj):
            o_ref[rows[z], j * GLA_DV:(j + 1) * GLA_DV] += intra[n] + inter[n]
            st_scr[z, j] = st[n] * jnp.exp(tot[z]) + upd[n]
        return carry

    lax.fori_loop(0, nc, step, 0)


def _gla(p, gw_pad, gb, *, n_batch, t_len, l_ctx):
    n = p.shape[0]
    spec = lambda col, w: pl.BlockSpec((t_len, w), lambda b, hp, col=col, w=w: (b, col // w + hp))
    kern = functools.partial(_gla_kernel, nc_ctx=l_ctx // CHUNK)
    return pl.pallas_call(
        kern,
        out_shape=jax.ShapeDtypeStruct((n, GLA_HEADS * GLA_DV), F32),
        grid=(n_batch, GLA_HEADS // 2),
        in_specs=[spec(A_Q, LANES), spec(A_K, LANES), spec(A_V, 2 * GLA_DV),
                  pl.BlockSpec((t_len, LANES), lambda b, hp: (b, A_GLR // LANES)),
                  pl.BlockSpec((2, LANES, LANES), lambda b, hp: (0, 0, hp)),
                  pl.BlockSpec((2, 1, LANES), lambda b, hp: (0, 0, hp))],
        out_specs=pl.BlockSpec((t_len, 2 * GLA_DV), lambda b, hp: (b, hp)),
        scratch_shapes=[pltpu.VMEM((2, 2, GLA_DV, LANES), F32)],
        compiler_params=_cparams(("parallel", "parallel")),
        name="gla_scan",
    )(p, p, p, p, gw_pad, gb)


def _gdn_kernel(q_ref, k_ref, v_ref, ba_ref, cwq_ref, cwk_ref, cwv_ref, ad_ref, o_ref,
                u_scr, kd_scr, w_scr, qg_scr, a_scr, dec_scr, s_scr, *, nc_ctx, l_ctx, group):
    t_len = q_ref.shape[0]
    nc = t_len // CHUNK
    h = pl.program_id(1)
    masks = _tri_masks(CHUNK)
    lane = lax.broadcasted_iota(jnp.int32, (1, LANES), 1)
    halo = 8
    pad = (GDN_CONV - 1) // 2
    ones_l0 = jnp.broadcast_to((lane == 0).astype(F32), (CHUNK, LANES))

    def conv(src_ref, cw_ref, r0):
        top = src_ref[pl.ds(pl.multiple_of(jnp.maximum(r0 - halo, 0), halo), halo), :]
        bot = src_ref[pl.ds(pl.multiple_of(jnp.minimum(r0 + CHUNK, t_len - halo), halo), halo), :]
        win = jnp.concatenate([top, src_ref[pl.ds(r0, CHUNK), :], bot], axis=0)
        row = r0 + lax.broadcasted_iota(jnp.int32, (CHUNK, 1), 0)
        acc = jnp.zeros((CHUNK, LANES), F32)
        for j in range(GDN_CONV):
            d = j - pad
            ok = (((row + d) >= l_ctx) == (row >= l_ctx)) & (row + d >= 0) & (row + d < t_len)
            acc = acc + jnp.where(ok, win[halo + d:halo + d + CHUNK, :], 0.0) * cw_ref[j:j + 1, :]
        return _silu(acc)

    def l2n(y):
        return y * lax.rsqrt(jnp.sum(y * y, axis=-1, keepdims=True) + 1e-6)

    def prep(gi, carry):
        per_chunk, chains = [], []
        for cc in range(group):
            c = gi * group + cc
            r0 = pl.multiple_of(c * CHUNK, CHUNK)
            q = l2n(conv(q_ref, cwq_ref, r0)) * GDN_DK ** -0.5
            k = l2n(conv(k_ref, cwk_ref, r0))
            v = conv(v_ref, cwv_ref, r0)
            ba = ba_ref[pl.ds(r0, CHUNK), :]
            beta_all = jax.nn.sigmoid(ba)
            g_all = -jnp.exp(ad_ref[0:1, :]) * _softplus(ba + ad_ref[1:2, :])
            per_chunk.append((c, r0, q, k, v, beta_all, g_all))
        grams = []
        for (_, _, q, k, _, _, _) in per_chunk:
            kbf = k.astype(BF16)
            grams.append((_dot_nt(kbf, kbf), _dot_nt(q.astype(BF16), kbf)))
        gcs = []
        for (_, _, _, _, _, beta_all, g_all) in per_chunk:
            for z in range(2):
                col = z * GDN_HEADS + h
                beta = jnp.sum(jnp.where(lane == col, beta_all, 0.0), axis=-1, keepdims=True)
                g = jnp.sum(jnp.where(lane == 2 * GDN_HEADS + col, g_all, 0.0), axis=-1, keepdims=True)
                gcs.append((beta, _dot_sel(masks[z][0], jnp.broadcast_to(g, (CHUNK, LANES)))))
        grows = [_dot_sel(ones_l0, gc, nt=True) for (_, gc) in gcs]
        xs, npows = [], []
        for ci, (c, r0, q, k, v, _, _) in enumerate(per_chunk):
            kk, qk = grams[ci]
            for z in range(2):
                incl, strict = masks[z]
                beta, gc = gcs[2 * ci + z]
                dmat = jnp.exp(jnp.where(incl, gc[:, :CHUNK] - grows[2 * ci + z], -jnp.inf))
                tot = gc[CHUNK - 1:CHUNK, :] if z == 0 else gc[0:1, :]
                rows = pl.ds(r0, CHUNK)
                a_scr[z, rows, :] = jnp.where(incl, qk * dmat, 0.0).astype(BF16)
                qg_scr[z, rows, :] = (q * jnp.exp(gc)).astype(BF16)
                kd_scr[z, rows, :] = k * jnp.exp(tot - gc)
                dec_scr[z, pl.ds(pl.multiple_of(c * halo, halo), halo), :] = jnp.broadcast_to(jnp.exp(tot), (halo, LANES))
                npows.append((-jnp.where(strict, beta * kk * dmat, 0.0)).astype(BF16))
                xs.append(jnp.concatenate([v * beta, k * (beta * jnp.exp(gc))], axis=1))
        for lvl in range(6):
            xs = [x + _dot(npw, x.astype(BF16)) for x, npw in zip(xs, npows)]
            if lvl < 5:
                npows = [_dot(npw, npw).astype(BF16) for npw in npows]
        for ci, (c, r0, _, _, _, _, _) in enumerate(per_chunk):
            for z in range(2):
                x = xs[2 * ci + z]
                u_scr[z, pl.ds(r0, CHUNK), :] = x[:, :GDN_DV]
                w_scr[z, pl.ds(r0, CHUNK), :] = x[:, GDN_DV:].astype(BF16)
        return carry

    lax.fori_loop(0, nc // group, prep, 0)

    o_ref[...] = jnp.zeros(o_ref.shape, F32)
    s_scr[...] = jnp.zeros(s_scr.shape, F32)

    def step(i, carry):
        cs = (i, _bwd_chunk(i, nc_ctx, nc))
        rows = [pl.ds(pl.multiple_of(c * CHUNK, CHUNK), CHUNK) for c in cs]
        s = [s_scr[z] for z in range(2)]
        sb = [s[z].astype(BF16) for z in range(2)]
        kdt = [kd_scr[z, rows[z], :].T.astype(BF16) for z in range(2)]
        ws = [_dot(w_scr[z, rows[z], :], sb[z]) for z in range(2)]
        qs = [_dot(qg_scr[z, rows[z], :], sb[z]) for z in range(2)]
        v_new = [(u_scr[z, rows[z], :] - ws[z]).astype(BF16) for z in range(2)]
        upd = [_dot(kdt[z], v_new[z]) for z in range(2)]
        av = [_dot(a_scr[z, rows[z], :], v_new[z]) for z in range(2)]
        for z in range(2):
            dec = dec_scr[z, pl.ds(pl.multiple_of(cs[z] * halo, halo), 1), :]
            s_scr[z] = s[z] * dec + upd[z]
            o_ref[rows[z], :] += qs[z] + av[z]
        return carry

    lax.fori_loop(0, nc, step, 0)


def _gdn(p, conv_w, ad, *, n_batch, t_len, l_ctx):
    n = p.shape[0]
    spec = lambda col: pl.BlockSpec((t_len, LANES), lambda b, h, col=col: (b, col // LANES + h))
    cspec = lambda off: pl.BlockSpec((GDN_CONV, LANES), lambda b, h, off=off: (0, off + h))
    nc = t_len // CHUNK
    group = next(g for g in (4, 2, 1) if nc % g == 0)
    kern = functools.partial(_gdn_kernel, nc_ctx=l_ctx // CHUNK, l_ctx=l_ctx, group=group)
    return pl.pallas_call(
        kern,
        out_shape=jax.ShapeDtypeStruct((n, GDN_HEADS * GDN_DV), F32),
        grid=(n_batch, GDN_HEADS),
        in_specs=[spec(B_Q), spec(B_K), spec(B_V),
                  pl.BlockSpec((t_len, LANES), lambda b, h: (b, B_BA // LANES)),
                  cspec(0), cspec(GDN_HEADS), cspec(2 * GDN_HEADS),
                  pl.BlockSpec((8, LANES), lambda b, h: (0, 0))],
        out_specs=pl.BlockSpec((t_len, GDN_DV), lambda b, h: (b, h)),
        scratch_shapes=[pltpu.VMEM((2, t_len, GDN_DV), F32),
                        pltpu.VMEM((2, t_len, GDN_DK), F32),
                        pltpu.VMEM((2, t_len, GDN_DK), BF16),
                        pltpu.VMEM((2, t_len, GDN_DK), BF16),
                        pltpu.VMEM((2, t_len, CHUNK), BF16),
                        pltpu.VMEM((2, nc * 8, LANES), F32),
                        pltpu.VMEM((2, GDN_DK, GDN_DV), F32)],
        compiler_params=_cparams(("parallel", "parallel")),
        name="gdn_scan",
    )(p, p, p, p, conv_w, conv_w, conv_w, ad)


def _rope_kernel(x_ref, cos_ref, sa_ref, sb_ref, o_ref):
    scale = jnp.where(pl.program_id(1) == 0, DIFF_DH ** -0.5, 1.0)
    cos, sa, sb = cos_ref[...] * scale, sa_ref[...] * scale, sb_ref[...] * scale
    for hh in range(DIFF_HEADS):
        x = x_ref[:, hh * LANES:(hh + 1) * LANES]
        y = x * cos + pltpu.roll(x, LANES - 16, 1) * sa + pltpu.roll(x, 16, 1) * sb
        o_ref[:, hh * LANES:(hh + 1) * LANES] = y.astype(BF16)


def _rope(p, cos_t, sa_t, sb_t, *, tm):
    n = p.shape[0]
    tblocks = cos_t.shape[0] // tm
    wq = DIFF_HEADS * LANES
    tspec = pl.BlockSpec((tm, LANES), lambda i, j: (i % tblocks, 0))
    return pl.pallas_call(
        _rope_kernel,
        out_shape=jax.ShapeDtypeStruct((n, 2 * wq), BF16),
        grid=(n // tm, 2),
        in_specs=[pl.BlockSpec((tm, wq), lambda i, j: (i, C_Q // wq + j)), tspec, tspec, tspec],
        out_specs=pl.BlockSpec((tm, wq), lambda i, j: (i, j)),
        compiler_params=_cparams(("parallel", "parallel")),
        name="rope",
    )(p, cos_t, sa_t, sb_t)


def _rope_tables(t_len, l_ctx):
    pos = jnp.arange(t_len - l_ctx)
    row = (pos // GRID_W).astype(F32)
    colp = (pos % GRID_W).astype(F32)
    half = DIFF_DH // 2
    inv = 1.0 / (ROPE_THETA ** (jnp.arange(0, half, 2, dtype=F32) / half))
    lane = jnp.arange(LANES)
    part = (lane % DIFF_DH) // half
    upper = (lane % half) // (half // 2)
    ang = jnp.where(part[None, :] == 0, row[:, None], colp[:, None]) * inv[lane % (half // 2)][None, :]
    cos, sin = jnp.cos(ang), jnp.sin(ang)
    sa = jnp.where(upper[None, :] == 0, -sin, 0.0)
    sb = jnp.where(upper[None, :] == 1, sin, 0.0)
    ident = lambda v, fill: jnp.concatenate([jnp.full((l_ctx, LANES), fill, F32), v.astype(F32)], axis=0)
    return ident(cos, 1.0), ident(sa, 0.0), ident(sb, 0.0)


def _attn_kernel(q_ref, k_ref, v_ref, lam_ref, nw_ref, o_ref, *, l_ctx, lambda_init):
    tq = q_ref.shape[0]
    t_len = k_ref.shape[0]
    lv = lam_ref[...]
    lam = (jnp.exp(jnp.sum(lv[0:1] * lv[1:2], axis=-1, keepdims=True))
           - jnp.exp(jnp.sum(lv[2:3] * lv[3:4], axis=-1, keepdims=True)) + lambda_init)
    lane = lax.broadcasted_iota(jnp.int32, (1, LANES), 1)

    def attend(kv_len):
        q = q_ref[...]
        k = k_ref[0:kv_len, :]
        v = v_ref[0:kv_len, :].astype(BF16)
        probs = []
        for m in range(2):
            qm = jnp.where((lane >= DIFF_DH) == (m == 1), q, jnp.zeros_like(q))
            s = _dot_nt(qm, k)
            e = jnp.exp(s - jnp.max(s, axis=-1, keepdims=True))
            probs.append(e * (1.0 / jnp.sum(e, axis=-1, keepdims=True)))
        w = probs[0] - lam * probs[1]
        o = _dot(w.astype(BF16), v)
        o_ref[...] = _rms(o, DIFF_EPS) * nw_ref[...] * (1.0 - lambda_init)

    is_ctx = pl.program_id(2) < l_ctx // tq
    pl.when(is_ctx)(lambda: attend(l_ctx))
    pl.when(jnp.logical_not(is_ctx))(lambda: attend(t_len))


def _diff_attn(qk, p, lam_vecs, norm_w, *, n_batch, t_len, l_ctx, lambda_init):
    n = p.shape[0]
    tq = min(l_ctx, 256)
    nt = t_len // tq
    kern = functools.partial(_attn_kernel, l_ctx=l_ctx, lambda_init=lambda_init)
    return pl.pallas_call(
        kern,
        out_shape=jax.ShapeDtypeStruct((n, DIFF_HEADS * DIFF_DV), F32),
        grid=(n_batch, DIFF_HEADS, nt),
        in_specs=[pl.BlockSpec((tq, LANES), lambda b, h, t: (b * nt + t, h)),
                  pl.BlockSpec((t_len, LANES), lambda b, h, t: (b, DIFF_HEADS + h)),
                  pl.BlockSpec((t_len, LANES), lambda b, h, t: (b, C_V // LANES + h)),
                  pl.BlockSpec((4, DIFF_DH), lambda b, h, t: (0, 0)),
                  pl.BlockSpec((1, DIFF_DV), lambda b, h, t: (0, 0))],
        out_specs=pl.BlockSpec((tq, DIFF_DV), lambda b, h, t: (b * nt + t, h)),
        compiler_params=_cparams(("parallel", "parallel", "parallel")),
        name="diff_attn",
    )(qk, qk, p, lam_vecs, norm_w.reshape(1, DIFF_DV))


def _out_proj_kernel(oa_ref, oga_ref, ob_ref, ogb_ref, yd_ref, anw_ref, bnw_ref, w_ref, x_ref, mod_ref,
                     n2_ref, rw_ref, rb_ref, xo_ref, h2_ref, te_ref, tg_ref, *, subs_per_batch, n_batch):
    m = mod_ref[_group_of(pl.program_id(0), subs_per_batch, n_batch)]
    pieces = []
    for o_ref, og_ref, nw_ref, heads in ((oa_ref, oga_ref, anw_ref, GLA_HEADS), (ob_ref, ogb_ref, bnw_ref, GDN_HEADS)):
        for hh in range(heads):
            sl = slice(hh * LANES, (hh + 1) * LANES)
            pieces.append((_rms(o_ref[:, sl], NORM_EPS) * nw_ref[...] * _silu(og_ref[:, sl])).astype(BF16))
    pieces.append(yd_ref[...].astype(BF16))
    mix = jnp.concatenate(pieces, axis=1)
    x_new = x_ref[...] + m[2:3] * _dot(mix, w_ref[...])
    xo_ref[...] = x_new
    h2 = _rms(x_new, NORM_EPS) * n2_ref[...] * (1.0 + m[4:5]) + m[3:4]
    h2_ref[...] = h2
    lg = _dot3(h2, rw_ref[...]) + rb_ref[...]
    lane = lax.broadcasted_iota(jnp.int32, lg.shape, 1)
    lane_f = lane.astype(F32)
    vals, idxs = [], []
    for _ in range(TOP_K):
        mx = jnp.max(lg, axis=-1, keepdims=True)
        ix = jnp.min(jnp.where(lg == mx, lane_f, float(LANES)), axis=-1, keepdims=True)
        vals.append(mx)
        idxs.append(ix)
        lg = jnp.where(lane_f == ix, -jnp.inf, lg)
    ex = [jnp.exp(v - vals[0]) for v in vals]
    inv_den = 1.0 / (ex[0] + ex[1] + ex[2] + ex[3])
    te = jnp.zeros(lg.shape, F32)
    tg = jnp.zeros(lg.shape, F32)
    for kk in range(TOP_K):
        te = jnp.where(lane == kk, idxs[kk], te)
        tg = jnp.where(lane == kk, ex[kk] * inv_den, tg)
    te_ref[...] = te.astype(jnp.int32)
    tg_ref[...] = tg


def _out_proj(oa, ob, yd, p, gla_nw, gdn_nw, w_out, xt, mod, norm2_w, rw_pad, rb_pad, *, tm, subs_per_batch, n_batch):
    n, d = xt.shape
    kern = functools.partial(_out_proj_kernel, subs_per_batch=subs_per_batch, n_batch=n_batch)
    const = lambda shape: pl.BlockSpec(shape, lambda i: (0,) * len(shape))
    wa, wb, wd = GLA_HEADS * GLA_DV, GDN_HEADS * GDN_DV, DIFF_HEADS * DIFF_DV
    return pl.pallas_call(
        kern,
        out_shape=(jax.ShapeDtypeStruct((n, d), F32), jax.ShapeDtypeStruct((n, d), F32),
                   jax.ShapeDtypeStruct((n, LANES), jnp.int32), jax.ShapeDtypeStruct((n, LANES), F32)),
        grid=(n // tm,),
        in_specs=[pl.BlockSpec((tm, wa), lambda i: (i, 0)),
                  pl.BlockSpec((tm, wa), lambda i: (i, A_OG // wa)),
                  pl.BlockSpec((tm, wb), lambda i: (i, 0)),
                  pl.BlockSpec((tm, wb), lambda i: (i, B_OG // wb)),
                  pl.BlockSpec((tm, wd), lambda i: (i, 0)),
                  const((1, LANES)), const((1, LANES)), const((D_MIX, d)),
                  pl.BlockSpec((tm, d), lambda i: (i, 0)),
                  const((MOD_ROWS, 6, d)), const((1, d)), const((d, LANES)), const((1, LANES))],
        out_specs=(pl.BlockSpec((tm, d), lambda i: (i, 0)), pl.BlockSpec((tm, d), lambda i: (i, 0)),
                   pl.BlockSpec((tm, LANES), lambda i: (i, 0)), pl.BlockSpec((tm, LANES), lambda i: (i, 0))),
        compiler_params=_cparams(("parallel",)),
        name="out_proj",
    )(oa, p, ob, p, yd, gla_nw.reshape(1, LANES), gdn_nw.reshape(1, LANES), w_out, xt, mod,
      norm2_w.reshape(1, d), rw_pad, rb_pad)


MOE_SUB = 256
PERM = 256


MOE_UNROLL = 4


def _for_sub_blocks(n, blocks):
    groups = n // MOE_UNROLL

    def body(i, carry):
        blocks(i * MOE_UNROLL, MOE_UNROLL)
        return carry

    lax.fori_loop(0, groups, body, 0)
    done = groups * MOE_UNROLL
    step = MOE_UNROLL // 2
    while step:
        take = ((n - done) & step) != 0
        pl.when(take)(functools.partial(blocks, done, step))
        done = done + jnp.where(take, step, 0)
        step //= 2


def _moe1_kernel(ve_ref, vn_ref, nu_ref, x_ref, w_ref, b_ref, perm_ref, o_ref, wp_scr):
    v = pl.program_id(0)
    tn = w_ref.shape[3]

    @pl.when(vn_ref[v] > 0)
    def _():
        for c in range(tn // PERM):
            wc = w_ref[0, 0, :, c * PERM:(c + 1) * PERM].astype(BF16)
            wp_scr[:, c * PERM:(c + 1) * PERM] = _dot(wc, perm_ref[...]).astype(BF16)

    def blocks(s0, count):
        rows = [pl.ds(pl.multiple_of((s0 + u) * MOE_SUB, MOE_SUB), MOE_SUB) for u in range(count)]
        hids = [_dot(x_ref[r, :].astype(BF16), wp_scr[...]) for r in rows]
        for r, hid in zip(rows, hids):
            hid = hid + b_ref[0]
            for c in range(tn // PERM):
                glu = jnp.minimum(hid[:, c * PERM:c * PERM + PERM // 2], SWIGLU_LIMIT)
                lin = jnp.clip(hid[:, c * PERM + PERM // 2:(c + 1) * PERM], -SWIGLU_LIMIT, SWIGLU_LIMIT)
                act = glu * jax.nn.sigmoid(SWIGLU_ALPHA * glu) * (lin + 1.0)
                o_ref[r, c * (PERM // 2):(c + 1) * (PERM // 2)] = act.astype(BF16)

    _for_sub_blocks(vn_ref[v], blocks)


def _moe2_kernel(ve_ref, vn_ref, nu_ref, a_ref, w_ref, b_ref, o_ref, wb_scr):
    v = pl.program_id(0)

    @pl.when(vn_ref[v] > 0)
    def _():
        wb_scr[...] = w_ref[0, 0].astype(BF16)

    def blocks(s0, count):
        rows = [pl.ds(pl.multiple_of((s0 + u) * MOE_SUB, MOE_SUB), MOE_SUB) for u in range(count)]
        outs = [_dot(a_ref[r, :], wb_scr[...]) for r in rows]
        for r, out in zip(rows, outs):
            o_ref[r, :] = out + b_ref[0]

    _for_sub_blocks(vn_ref[v], blocks)


def _moe_experts(xs, vis_e, vis_n, n_used, w1, b1p, w2, b2, perm, *, layer, rows_per_visit):
    _, n_exp, d, dh2 = w1.shape
    dh = dh2 // 2
    r = rows_per_visit
    n_vis = xs.shape[0] // r

    def maps(n_j):
        live = lambda v, nu: v < nu[0]
        jj = lambda v, j, nu: jnp.where(live(v, nu), j, n_j - 1)
        x_map = lambda v, j, ve, vn, nu: (jnp.minimum(v, nu[0] - 1), 0)
        w_map = lambda v, j, ve, vn, nu: (layer, ve[v], 0, jj(v, j, nu))
        b_map = lambda v, j, ve, vn, nu: (ve[v], 0, jj(v, j, nu))
        o_map = lambda v, j, ve, vn, nu: (jnp.where(live(v, nu), v, n_vis), jj(v, j, nu))
        return x_map, w_map, b_map, o_map

    tn1 = _pick(dh2, 512)
    x_map, w_map, b_map, o_map = maps(dh2 // tn1)
    act = pl.pallas_call(
        _moe1_kernel,
        out_shape=jax.ShapeDtypeStruct(((n_vis + 1) * r, dh), BF16),
        grid_spec=pltpu.PrefetchScalarGridSpec(
            num_scalar_prefetch=3, grid=(n_vis, dh2 // tn1),
            in_specs=[pl.BlockSpec((r, d), x_map),
                      pl.BlockSpec((1, 1, d, tn1), w_map),
                      pl.BlockSpec((1, 1, tn1), b_map),
                      pl.BlockSpec((PERM, PERM), lambda v, j, ve, vn, nu: (0, 0))],
            out_specs=pl.BlockSpec((r, tn1 // 2), o_map),
            scratch_shapes=[pltpu.VMEM((d, tn1), BF16)]),
        compiler_params=_cparams(("arbitrary", "arbitrary")),
        name="moe_up",
    )(vis_e, vis_n, n_used, xs, w1, b1p, perm)
    tn2 = _pick(d, 512)
    x_map, w_map, b_map, o_map = maps(d // tn2)
    return pl.pallas_call(
        _moe2_kernel,
        out_shape=jax.ShapeDtypeStruct(((n_vis + 1) * r, d), F32),
        grid_spec=pltpu.PrefetchScalarGridSpec(
            num_scalar_prefetch=3, grid=(n_vis, d // tn2),
            in_specs=[pl.BlockSpec((r, dh), x_map),
                      pl.BlockSpec((1, 1, dh, tn2), w_map),
                      pl.BlockSpec((1, 1, tn2), b_map)],
            out_specs=pl.BlockSpec((r, tn2), o_map),
            scratch_shapes=[pltpu.VMEM((dh, tn2), BF16)]),
        compiler_params=_cparams(("arbitrary", "arbitrary")),
        name="moe_down",
    )(vis_e, vis_n, n_used, act, w2, b2.reshape(n_exp, 1, d))


def _perm_matrix():
    src = jnp.arange(PERM)
    dst = jnp.where(src % 2 == 0, src // 2, PERM // 2 + src // 2)
    return (dst[:, None] == jnp.arange(PERM)[None, :]).astype(BF16)


def _deinterleave_bias(b1):
    n_exp, dh2 = b1.shape
    b = b1.reshape(n_exp, dh2 // PERM, PERM // 2, 2)
    return jnp.concatenate([b[..., 0], b[..., 1]], axis=-1).reshape(n_exp, 1, dh2)


def _routing(top_e, n_tok, rows_per_visit):
    r = rows_per_visit
    n_assign = n_tok * TOP_K
    n_vis = n_assign // r + N_EXPERTS
    flat_e = top_e.reshape(-1)
    onehot = (flat_e[:, None] == jnp.arange(N_EXPERTS)[None, :]).astype(jnp.int32)
    rank = jnp.take_along_axis(jnp.cumsum(onehot, axis=0) - onehot, flat_e[:, None], axis=1)[:, 0]
    counts = jnp.sum(onehot, axis=0)
    seg_vis = (counts + r - 1) // r
    vis_end = jnp.cumsum(seg_vis)
    vis_start = vis_end - seg_vis
    dest = vis_start[flat_e] * r + rank
    v = jnp.arange(n_vis)
    n_used = vis_end[-1]
    vc = jnp.minimum(v, n_used - 1)
    vis_e = jnp.minimum(jnp.searchsorted(vis_end, vc, side="right"), N_EXPERTS - 1).astype(jnp.int32)
    valid = jnp.clip(counts[vis_e] - (vc - vis_start[vis_e]) * r, 0, r)
    vis_n = jnp.where(v < n_used, (valid + MOE_SUB - 1) // MOE_SUB, 0).astype(jnp.int32)
    tail = jnp.minimum(vis_start * r + counts // MOE_SUB * MOE_SUB, (n_vis * r) - MOE_SUB).astype(jnp.int32)
    return dest.astype(jnp.int32), tail, vis_e, vis_n, n_used.reshape(1).astype(jnp.int32), n_vis


def _dispatch_kernel(idx_ref, tail_ref, h_ref, xs_ref, zero_scr, zsem, sem):
    tm = h_ref.shape[0]

    def zero_copy(e):
        start = pl.multiple_of(tail_ref[0, e], MOE_SUB)
        return pltpu.make_async_copy(zero_scr, xs_ref.at[pl.ds(start, MOE_SUB), :], zsem)

    @pl.when(pl.program_id(0) == 0)
    def _():
        zero_scr[...] = jnp.zeros(zero_scr.shape, F32)
        for e in range(N_EXPERTS):
            zero_copy(e).start()
        for e in range(N_EXPERTS):
            zero_copy(e).wait()

    def row_copy(t, k, row):
        return pltpu.make_async_copy(h_ref.at[pl.ds(t, 1), :], xs_ref.at[pl.ds(row, 1), :], sem)

    def issue(t, carry):
        for k in range(TOP_K):
            row_copy(t, k, idx_ref[0, 0, t * TOP_K + k]).start()
        return carry

    def drain(t, carry):
        for k in range(TOP_K):
            row_copy(t, k, 0).wait()
        return carry

    lax.fori_loop(0, tm, issue, 0)
    lax.fori_loop(0, tm, drain, 0)


def _dispatch(h2, dest, tail, *, n_rows, sub):
    n, d = h2.shape
    tm = min(sub, 128)
    return pl.pallas_call(
        _dispatch_kernel,
        out_shape=jax.ShapeDtypeStruct((n_rows, d), F32),
        grid=(n // tm,),
        in_specs=[pl.BlockSpec((1, 1, tm * TOP_K), lambda i: (i, 0, 0), memory_space=pltpu.SMEM),
                  pl.BlockSpec(memory_space=pltpu.SMEM),
                  pl.BlockSpec((tm, d), lambda i: (i, 0))],
        out_specs=pl.BlockSpec(memory_space=pl.ANY),
        scratch_shapes=[pltpu.VMEM((MOE_SUB, d), F32), pltpu.SemaphoreType.DMA(()), pltpu.SemaphoreType.DMA(())],
        compiler_params=_cparams(("arbitrary",)),
        name="moe_dispatch",
    )(dest.reshape(n // tm, 1, tm * TOP_K), tail.reshape(1, N_EXPERTS), h2)


def _combine_kernel(idx_ref, x_ref, g_ref, mod_ref, fw_ref, yb_ref, o_ref, buf, sem, *,
                    sub, subs_per_batch, n_batch, final):
    tm = x_ref.shape[0]

    def row_copy(t, k, row):
        return pltpu.make_async_copy(yb_ref.at[pl.ds(row, 1), :], buf.at[k, pl.ds(t, 1), :], sem)

    def issue(t, carry):
        for k in range(TOP_K):
            row_copy(t, k, idx_ref[0, 0, t * TOP_K + k]).start()
        return carry

    def drain(t, carry):
        for k in range(TOP_K):
            row_copy(t, k, 0).wait()
        return carry

    lax.fori_loop(0, tm, issue, 0)
    lax.fori_loop(0, tm, drain, 0)
    g = g_ref[...]
    y = g[:, 0:1] * buf[0]
    for k in range(1, TOP_K):
        y = y + g[:, k:k + 1] * buf[k]
    m = mod_ref[_group_of((pl.program_id(0) * tm) // sub, subs_per_batch, n_batch)]
    x_new = x_ref[...] + m[5:6] * y
    o_ref[...] = _rms(x_new, NORM_EPS) * fw_ref[...] if final else x_new


def _combine(xt, yb, dest, top_g, mod, final_w, *, sub, subs_per_batch, n_batch, final):
    n, d = xt.shape
    tm = min(sub, 128)
    kern = functools.partial(_combine_kernel, sub=sub, subs_per_batch=subs_per_batch, n_batch=n_batch, final=final)
    row = pl.BlockSpec((tm, d), lambda i: (i, 0))
    return pl.pallas_call(
        kern,
        out_shape=jax.ShapeDtypeStruct((n, d), F32),
        grid=(n // tm,),
        in_specs=[pl.BlockSpec((1, 1, tm * TOP_K), lambda i: (i, 0, 0), memory_space=pltpu.SMEM),
                  row, pl.BlockSpec((tm, LANES), lambda i: (i, 0)),
                  pl.BlockSpec((MOD_ROWS, 6, d), lambda i: (0, 0, 0)), pl.BlockSpec((1, d), lambda i: (0, 0)),
                  pl.BlockSpec(memory_space=pl.ANY)],
        out_specs=row,
        scratch_shapes=[pltpu.VMEM((TOP_K, tm, d), F32), pltpu.SemaphoreType.DMA(())],
        compiler_params=_cparams(("arbitrary",)),
        name="moe_combine",
    )(dest.reshape(n // tm, 1, tm * TOP_K), xt, top_g, mod, final_w.reshape(1, d), yb)


def _pad_cols(w, cuts):
    parts = []
    for start, width, padded in cuts:
        parts.append(w[:, start:start + width])
        if padded > width:
            parts.append(jnp.zeros((w.shape[0], padded - width), w.dtype))
    return jnp.concatenate(parts, axis=1)


def kernel(x, c, ctx, c_ctx, w_mod, b_mod, norm1_w, w_in, gla_gate_w, gla_gate_b, gla_norm_w, gdn_conv_w,
           gdn_a_log, gdn_dt_bias, gdn_norm_w, diff_lambda, diff_norm_w, w_out, norm2_w, router_w, router_b,
           expert_w1, expert_b1, expert_w2, expert_b2, final_norm_w):
    n_batch, s_len, d = x.shape
    l_ctx = ctx.shape[1]
    depth = w_mod.shape[0]
    t_len = l_ctx + s_len
    n_tok = n_batch * t_len
    subs_per_batch = t_len // l_ctx
    assert s_len % l_ctx == 0 and l_ctx % CHUNK == 0 and n_batch < MOD_ROWS and l_ctx % 16 == 0
    rows_per_visit = 2048

    xt = jnp.concatenate([ctx, x], axis=1).reshape(n_tok, d)
    cvec = jnp.concatenate([c, c_ctx[None, :], jnp.zeros((MOD_ROWS - n_batch - 1, d), F32)], axis=0)
    mod_all = _modulation(cvec, w_mod, b_mod).reshape(depth, MOD_ROWS, 6, d)
    cos_t, sa_t, sb_t = _rope_tables(t_len, l_ctx)
    perm = _perm_matrix()
    a_w = GLA_HEADS * (2 * GLA_DK + 2 * GLA_DV)
    b_w = GDN_HEADS * (2 * GDN_DK + 2 * GDN_DV)
    c_w = DIFF_HEADS * (4 * DIFF_DH + DIFF_DV)
    glr0, b0 = a_w, a_w + 2 * GLA_RANK
    ba0 = b0 + b_w
    c0 = ba0 + 4 * GDN_HEADS
    cuts = ((0, a_w, a_w), (b0, b_w, b_w), (c0, c_w, c_w), (glr0, 2 * GLA_RANK, LANES), (ba0, 4 * GDN_HEADS, LANES))
    assert (a_w, a_w + b_w, a_w + b_w + c_w) == (B_Q, C_Q, A_GLR)

    for l in range(depth):
        lambda_init = 0.8 - 0.6 * math.exp(-0.3 * l)
        mod = mod_all[l]
        w_pad = _pad_cols(w_in[l], cuts).astype(BF16)
        p = _in_proj(xt, norm1_w[l], mod, w_pad, sub=l_ctx, subs_per_batch=subs_per_batch, n_batch=n_batch)

        gw = gla_gate_w[l]
        gw_pad = jnp.zeros((2, LANES, GLA_HEADS * GLA_DK), F32)
        gw_pad = gw_pad.at[0, 0:GLA_RANK].set(gw[0]).at[1, GLA_RANK:2 * GLA_RANK].set(gw[1])
        oa = _gla(p, gw_pad, gla_gate_b[l].reshape(2, 1, -1), n_batch=n_batch, t_len=t_len, l_ctx=l_ctx)

        ad = jnp.zeros((8, LANES), F32)
        ad = ad.at[0, 2 * GDN_HEADS:4 * GDN_HEADS].set(gdn_a_log[l].reshape(-1))
        ad = ad.at[1, 2 * GDN_HEADS:4 * GDN_HEADS].set(gdn_dt_bias[l].reshape(-1))
        ob = _gdn(p, gdn_conv_w[l], ad, n_batch=n_batch, t_len=t_len, l_ctx=l_ctx)

        qk = _rope(p, cos_t, sa_t, sb_t, tm=l_ctx)
        yd = _diff_attn(qk, p, diff_lambda[l], diff_norm_w[l], n_batch=n_batch, t_len=t_len, l_ctx=l_ctx,
                        lambda_init=lambda_init)

        rw_pad = jnp.zeros((d, LANES), F32).at[:, :N_EXPERTS].set(router_w[l])
        rb_pad = jnp.full((1, LANES), -1e30, F32).at[0, :N_EXPERTS].set(router_b[l])
        xt, h2, top_e, top_g = _out_proj(oa, ob, yd, p, gla_norm_w[l], gdn_norm_w[l], w_out[l].astype(BF16), xt, mod,
                                         norm2_w[l], rw_pad, rb_pad, tm=l_ctx, subs_per_batch=subs_per_batch,
                                         n_batch=n_batch)

        dest, tail, vis_e, vis_n, n_used, n_vis = _routing(top_e[:, :TOP_K], n_tok, rows_per_visit)
        xs = _dispatch(h2, dest, tail, n_rows=n_vis * rows_per_visit, sub=l_ctx)
        yb = _moe_experts(xs, vis_e, vis_n, n_used, expert_w1, _deinterleave_bias(expert_b1[l]),
                          expert_w2, expert_b2[l], perm, layer=l, rows_per_visit=rows_per_visit)
        xt = _combine(xt, yb, dest, top_g, mod, final_norm_w, sub=l_ctx, subs_per_batch=subs_per_batch,
                      n_batch=n_batch, final=(l == depth - 1))
    return xt.reshape(n_batch, t_len, d)[:, l_ctx:, :]
```

```python
import functools
import math

import jax
import jax.numpy as jnp
from jax import lax
from jax.experimental import pallas as pl
from jax.experimental.pallas import tpu as pltpu

F32 = jnp.float32
BF16 = jnp.bfloat16

GRID_W = 64
GLA_HEADS, GLA_DK, GLA_DV, GLA_RANK, GLA_GATE_NORM = 4, 64, 128, 16, 16.0
GDN_HEADS, GDN_DK, GDN_DV, GDN_CONV = 6, 128, 128, 5
DIFF_HEADS, DIFF_DH, DIFF_DV, DIFF_EPS = 6, 64, 128, 1e-5
ROPE_THETA = 10000.0
CHUNK = 64
N_EXPERTS, TOP_K = 32, 4
SWIGLU_ALPHA, SWIGLU_LIMIT = 1.702, 7.0
NORM_EPS = 1e-6
LANES = 128
MOD_ROWS = 8

A_Q, A_K, A_V, A_OG = 0, 256, 512, 1024
B_Q, B_K, B_V, B_OG = 1536, 2304, 3072, 3840
C_Q, C_K, C_V = 4608, 5376, 6144
A_GLR, B_BA = 6912, 7040
NP_IN = 7168
D_MIX = GLA_HEADS * GLA_DV + GDN_HEADS * GDN_DV + DIFF_HEADS * DIFF_DV

VMEM_LIMIT = 56 * 1024 * 1024

def _dot(a, b):
    return jnp.dot(a, b, preferred_element_type=F32)


def _dot_nt(a, b):
    return lax.dot_general(a, b, (((1,), (1,)), ((), ())), preferred_element_type=F32)


def _split3(x):
    hi = x.astype(BF16)
    r = x - hi.astype(F32)
    mid = r.astype(BF16)
    return hi, mid, (r - mid.astype(F32)).astype(BF16)


def _dot_sel(sel, x, nt=False):
    dot = _dot_nt if nt else _dot
    s = sel.astype(F32).astype(BF16)
    hi, mid, lo = _split3(x)
    return dot(s, hi) + dot(s, mid) + dot(s, lo)


def _dot3(a, b):
    a_hi = a.astype(BF16)
    b_hi = b.astype(BF16)
    a_lo = (a - a_hi.astype(F32)).astype(BF16)
    b_lo = (b - b_hi.astype(F32)).astype(BF16)
    return _dot(a_hi, b_hi) + _dot(a_hi, b_lo) + _dot(a_lo, b_hi)


def _pick(n, pref):
    if n <= pref:
        return n
    t = pref - pref % LANES
    while t > LANES and n % t:
        t -= LANES
    return t


def _silu(x):
    return x * jax.nn.sigmoid(x)


def _log_sigmoid(x):
    return jnp.minimum(x, 0.0) - jnp.log(1.0 + jnp.exp(-jnp.abs(x)))


def _softplus(x):
    return jnp.maximum(x, 0.0) + jnp.log(1.0 + jnp.exp(-jnp.abs(x)))


def _cparams(sem):
    return pltpu.CompilerParams(dimension_semantics=sem, vmem_limit_bytes=VMEM_LIMIT)


def _mod_kernel(c_ref, w_ref, b_ref, o_ref):
    a = _silu(c_ref[...])
    o_ref[0] = _dot(a.astype(BF16), w_ref[0].astype(BF16)) + b_ref[0]


def _modulation(cvec, w_mod, b_mod):
    depth, d, n6 = w_mod.shape
    tn = _pick(n6, 1024)
    return pl.pallas_call(
        _mod_kernel,
        out_shape=jax.ShapeDtypeStruct((depth, MOD_ROWS, n6), F32),
        grid=(depth, n6 // tn),
        in_specs=[pl.BlockSpec((MOD_ROWS, d), lambda l, j: (0, 0)),
                  pl.BlockSpec((1, d, tn), lambda l, j: (l, 0, j)),
                  pl.BlockSpec((1, 1, tn), lambda l, j: (l, 0, j))],
        out_specs=pl.BlockSpec((1, MOD_ROWS, tn), lambda l, j: (l, 0, j)),
        compiler_params=_cparams(("parallel", "parallel")),
        name="modulation",
    )(cvec, w_mod, b_mod.reshape(depth, 1, n6))


def _group_of(sub_block, subs_per_batch, n_batch):
    return jnp.where(sub_block % subs_per_batch == 0, n_batch, sub_block // subs_per_batch)


def _rms(x, eps):
    return x * lax.rsqrt(jnp.mean(x * x, axis=-1, keepdims=True) + eps)


def _in_proj_kernel(x_ref, nw_ref, mod_ref, w_ref, o_ref, h_scr, *, sub, subs_per_batch, n_batch):
    i = pl.program_id(0)

    @pl.when(pl.program_id(1) == 0)
    def _():
        for s in range(x_ref.shape[0] // sub):
            m = mod_ref[_group_of(i * (x_ref.shape[0] // sub) + s, subs_per_batch, n_batch)]
            y = _rms(x_ref[s * sub:(s + 1) * sub, :], NORM_EPS) * nw_ref[...]
            h_scr[s * sub:(s + 1) * sub, :] = (y * (1.0 + m[1:2]) + m[0:1]).astype(BF16)

    o_ref[...] = _dot(h_scr[...], w_ref[...])


def _in_proj(xt, norm_w, mod, w_pad, *, sub, subs_per_batch, n_batch):
    n, d = xt.shape
    npad = w_pad.shape[1]
    tm = sub * 2 if (n // sub) % 2 == 0 else sub
    tn = _pick(npad, 1024)
    kern = functools.partial(_in_proj_kernel, sub=sub, subs_per_batch=subs_per_batch, n_batch=n_batch)
    return pl.pallas_call(
        kern,
        out_shape=jax.ShapeDtypeStruct((n, npad), F32),
        grid=(n // tm, npad // tn),
        in_specs=[pl.BlockSpec((tm, d), lambda i, j: (i, 0)),
                  pl.BlockSpec((1, d), lambda i, j: (0, 0)),
                  pl.BlockSpec((MOD_ROWS, 6, d), lambda i, j: (0, 0, 0)),
                  pl.BlockSpec((d, tn), lambda i, j: (0, j))],
        out_specs=pl.BlockSpec((tm, tn), lambda i, j: (i, j)),
        scratch_shapes=[pltpu.VMEM((tm, d), BF16)],
        compiler_params=_cparams(("parallel", "arbitrary")),
        name="in_proj",
    )(xt, norm_w.reshape(1, d), mod, w_pad)


def _tri_masks(c):
    r = lax.broadcasted_iota(jnp.int32, (c, c), 0)
    s = lax.broadcasted_iota(jnp.int32, (c, c), 1)
    return ((s <= r, s < r), (s >= r, s > r))


def _bwd_chunk(i, nc_ctx, nc):
    return jnp.where(i < nc_ctx, nc_ctx - 1 - i, nc - 1 - (i - nc_ctx))


def _gla_kernel(q_ref, k_ref, v_ref, glr_ref, gw_ref, gb_ref, o_ref, st_scr, *, nc_ctx):
    t_len = q_ref.shape[0]
    nc = t_len // CHUNK
    masks = _tri_masks(CHUNK)
    lane = lax.broadcasted_iota(jnp.int32, (1, LANES), 1)
    head_mask = [(lane < GLA_DK).astype(F32), (lane >= GLA_DK).astype(F32)]
    o_ref[...] = jnp.zeros(o_ref.shape, F32)
    st_scr[...] = jnp.zeros(st_scr.shape, F32)

    def step(i, carry):
        cs = (i, _bwd_chunk(i, nc_ctx, nc))
        rows = [pl.ds(pl.multiple_of(c * CHUNK, CHUNK), CHUNK) for c in cs]
        pre = [_dot3(glr_ref[rows[z], :], gw_ref[z]) + gb_ref[z] for z in range(2)]
        b = [_dot_sel(masks[z][0], _log_sigmoid(pre[z]) / GLA_GATE_NORM) for z in range(2)]
        tot = [b[0][CHUNK - 1:CHUNK, :], b[1][0:1, :]]
        qd, kk, kd = [], [], []
        for z in range(2):
            k = k_ref[rows[z], :]
            qd.append(q_ref[rows[z], :] * GLA_DK ** -0.5 * jnp.exp(b[z]))
            kk.append((k * jnp.exp(-b[z])).astype(BF16))
            kd.append(k * jnp.exp(tot[z] - b[z]))
        zj = [(z, j) for z in range(2) for j in range(2)]
        qdj = [(qd[z] * head_mask[j]).astype(BF16) for z, j in zj]
        vj = [v_ref[rows[z], j * GLA_DV:(j + 1) * GLA_DV] for z, j in zj]
        st = [st_scr[z, j] for z, j in zj]
        att = [jnp.where(masks[z][0], _dot_nt(qdj[n], kk[z]), 0.0).astype(BF16) for n, (z, j) in enumerate(zj)]
        inter = [_dot_nt(qdj[n], st[n].astype(BF16)) for n in range(4)]
        intra = [_dot(att[n], vj[n].astype(BF16)) for n in range(4)]
        upd = [_dot(vj[n].T.astype(BF16), (kd[z] * head_mask[j]).astype(BF16)) for n, (z, j) in enumerate(zj)]
        for n, (z, j) in enumerate(zj):
            o_ref[rows[z], j * GLA_DV:(j + 1) * GLA_DV] += intra[n] + inter[n]
            st_scr[z, j] = st[n] * jnp.exp(tot[z]) + upd[n]
        return carry

    lax.fori_loop(0, nc, step, 0)


def _gla(p, gw_pad, gb, *, n_batch, t_len, l_ctx):
    n = p.shape[0]
    spec = lambda col, w: pl.BlockSpec((t_len, w), lambda b, hp, col=col, w=w: (b, col // w + hp))
    kern = functools.partial(_gla_kernel, nc_ctx=l_ctx // CHUNK)
    return pl.pallas_call(
        kern,
        out_shape=jax.ShapeDtypeStruct((n, GLA_HEADS * GLA_DV), F32),
        grid=(n_batch, GLA_HEADS // 2),
        in_specs=[spec(A_Q, LANES), spec(A_K, LANES), spec(A_V, 2 * GLA_DV),
                  pl.BlockSpec((t_len, LANES), lambda b, hp: (b, A_GLR // LANES)),
                  pl.BlockSpec((2, LANES, LANES), lambda b, hp: (0, 0, hp)),
                  pl.BlockSpec((2, 1, LANES), lambda b, hp: (0, 0, hp))],
        out_specs=pl.BlockSpec((t_len, 2 * GLA_DV), lambda b, hp: (b, hp)),
        scratch_shapes=[pltpu.VMEM((2, 2, GLA_DV, LANES), F32)],
        compiler_params=_cparams(("parallel", "parallel")),
        name="gla_scan",
    )(p, p, p, p, gw_pad, gb)


def _gdn_kernel(q_ref, k_ref, v_ref, ba_ref, cwq_ref, cwk_ref, cwv_ref, ad_ref, o_ref,
                m_scr, b_scr, qp_scr, dec_scr, s_scr, *, nc_ctx, l_ctx, group):
    t_len = q_ref.shape[0]
    nc = t_len // CHUNK
    h = pl.program_id(1)
    masks = _tri_masks(CHUNK)
    lane = lax.broadcasted_iota(jnp.int32, (1, LANES), 1)
    halo = 8
    pad = (GDN_CONV - 1) // 2
    ones_l0 = jnp.broadcast_to((lane == 0).astype(F32), (CHUNK, LANES))

    def conv(src_ref, cw_ref, r0):
        top = src_ref[pl.ds(pl.multiple_of(jnp.maximum(r0 - halo, 0), halo), halo), :]
        bot = src_ref[pl.ds(pl.multiple_of(jnp.minimum(r0 + CHUNK, t_len - halo), halo), halo), :]
        win = jnp.concatenate([top, src_ref[pl.ds(r0, CHUNK), :], bot], axis=0)
        row = r0 + lax.broadcasted_iota(jnp.int32, (CHUNK, 1), 0)
        acc = jnp.zeros((CHUNK, LANES), F32)
        for j in range(GDN_CONV):
            d = j - pad
            ok = (((row + d) >= l_ctx) == (row >= l_ctx)) & (row + d >= 0) & (row + d < t_len)
            acc = acc + jnp.where(ok, win[halo + d:halo + d + CHUNK, :], 0.0) * cw_ref[j:j + 1, :]
        return _silu(acc)

    def l2n(y):
        return y * lax.rsqrt(jnp.sum(y * y, axis=-1, keepdims=True) + 1e-6)

    def prep(gi, carry):
        per_chunk, chains = [], []
        for cc in range(group):
            c = gi * group + cc
            r0 = pl.multiple_of(c * CHUNK, CHUNK)
            q = l2n(conv(q_ref, cwq_ref, r0)) * GDN_DK ** -0.5
            k = l2n(conv(k_ref, cwk_ref, r0))
            v = conv(v_ref, cwv_ref, r0)
            ba = ba_ref[pl.ds(r0, CHUNK), :]
            beta_all = jax.nn.sigmoid(ba)
            g_all = -jnp.exp(ad_ref[0:1, :]) * _softplus(ba + ad_ref[1:2, :])
            per_chunk.append((c, r0, q, k, v, beta_all, g_all))
        grams = []
        for (_, _, q, k, _, _, _) in per_chunk:
            kbf = k.astype(BF16)
            grams.append((_dot_nt(kbf, kbf), _dot_nt(q.astype(BF16), kbf)))
        gcs = []
        for (_, _, _, _, _, beta_all, g_all) in per_chunk:
            for z in range(2):
                col = z * GDN_HEADS + h
                beta = jnp.sum(jnp.where(lane == col, beta_all, 0.0), axis=-1, keepdims=True)
                g = jnp.sum(jnp.where(lane == 2 * GDN_HEADS + col, g_all, 0.0), axis=-1, keepdims=True)
                gcs.append((beta, _dot_sel(masks[z][0], jnp.broadcast_to(g, (CHUNK, LANES)))))
        grows = [_dot_sel(ones_l0, gc, nt=True) for (_, gc) in gcs]
        xs, npows, aqk, qg, kdt, where = [], [], [], [], [], []
        for ci, (c, r0, q, k, v, _, _) in enumerate(per_chunk):
            kk, qk = grams[ci]
            for z in range(2):
                incl, strict = masks[z]
                beta, gc = gcs[2 * ci + z]
                dmat = jnp.exp(jnp.where(incl, gc[:, :CHUNK] - grows[2 * ci + z], -jnp.inf))
                tot = gc[CHUNK - 1:CHUNK, :] if z == 0 else gc[0:1, :]
                where.append((z, c, r0))
                aqk.append(jnp.where(incl, qk * dmat, 0.0).astype(BF16))
                qg.append(q * jnp.exp(gc))
                kdt.append((k * jnp.exp(tot - gc)).T.astype(BF16))
                dec_scr[z, pl.ds(pl.multiple_of(c * halo, halo), halo), :] = jnp.broadcast_to(jnp.exp(tot), (halo, LANES))
                npows.append((-jnp.where(strict, beta * kk * dmat, 0.0)).astype(BF16))
                xs.append(jnp.concatenate([v * beta, k * (beta * jnp.exp(gc))], axis=1))
        for lvl in range(6):
            xs = [x + _dot(npw, x.astype(BF16)) for x, npw in zip(xs, npows)]
            if lvl < 5:
                npows = [_dot(npw, npw).astype(BF16) for npw in npows]
        ub = [x[:, :GDN_DV].astype(BF16) for x in xs]
        wb = [x[:, GDN_DV:].astype(BF16) for x in xs]
        m_all = [_dot(kdt[n], wb[n]) for n in range(len(xs))]
        b_all = [_dot(kdt[n], ub[n]) for n in range(len(xs))]
        aw = [_dot(aqk[n], wb[n]) for n in range(len(xs))]
        au = [_dot(aqk[n], ub[n]) for n in range(len(xs))]
        for n, (z, c, r0) in enumerate(where):
            blk = pl.ds(pl.multiple_of(c * GDN_DK, GDN_DK), GDN_DK)
            m_scr[z, blk, :] = (-m_all[n]).astype(BF16)
            b_scr[z, blk, :] = b_all[n]
            qp_scr[z, pl.ds(r0, CHUNK), :] = (qg[n] - aw[n]).astype(BF16)
            o_ref[pl.ds(r0, CHUNK), :] += au[n]
        return carry

    o_ref[...] = jnp.zeros(o_ref.shape, F32)
    lax.fori_loop(0, nc // group, prep, 0)
    s_scr[...] = jnp.zeros(s_scr.shape, F32)

    def step(i, carry):
        cs = (i, _bwd_chunk(i, nc_ctx, nc))
        s = [s_scr[z] for z in range(2)]
        sb = [s[z].astype(BF16) for z in range(2)]
        blk = [pl.ds(pl.multiple_of(cs[z] * GDN_DK, GDN_DK), GDN_DK) for z in range(2)]
        rows = [pl.ds(pl.multiple_of(cs[z] * CHUNK, CHUNK), CHUNK) for z in range(2)]
        ms = [_dot(m_scr[z, blk[z], :], sb[z]) for z in range(2)]
        qs = [_dot(qp_scr[z, rows[z], :], sb[z]) for z in range(2)]
        for z in range(2):
            dec = dec_scr[z, pl.ds(pl.multiple_of(cs[z] * halo, halo), 1), :]
            s_scr[z] = s[z] * dec + ms[z] + b_scr[z, blk[z], :]
            o_ref[rows[z], :] += qs[z]
        return carry

    lax.fori_loop(0, nc, step, 0)


def _gdn(p, conv_w, ad, *, n_batch, t_len, l_ctx):
    n = p.shape[0]
    spec = lambda col: pl.BlockSpec((t_len, LANES), lambda b, h, col=col: (b, col // LANES + h))
    cspec = lambda off: pl.BlockSpec((GDN_CONV, LANES), lambda b, h, off=off: (0, off + h))
    nc = t_len // CHUNK
    group = next(g for g in (4, 2, 1) if nc % g == 0)
    kern = functools.partial(_gdn_kernel, nc_ctx=l_ctx // CHUNK, l_ctx=l_ctx, group=group)
    return pl.pallas_call(
        kern,
        out_shape=jax.ShapeDtypeStruct((n, GDN_HEADS * GDN_DV), F32),
        grid=(n_batch, GDN_HEADS),
        in_specs=[spec(B_Q), spec(B_K), spec(B_V),
                  pl.BlockSpec((t_len, LANES), lambda b, h: (b, B_BA // LANES)),
                  cspec(0), cspec(GDN_HEADS), cspec(2 * GDN_HEADS),
                  pl.BlockSpec((8, LANES), lambda b, h: (0, 0))],
        out_specs=pl.BlockSpec((t_len, GDN_DV), lambda b, h: (b, h)),
        scratch_shapes=[pltpu.VMEM((2, nc * GDN_DK, GDN_DK), BF16),
                        pltpu.VMEM((2, nc * GDN_DK, GDN_DV), F32),
                        pltpu.VMEM((2, t_len, GDN_DK), BF16),
                        pltpu.VMEM((2, nc * 8, LANES), F32),
                        pltpu.VMEM((2, GDN_DK, GDN_DV), F32)],
        compiler_params=_cparams(("parallel", "parallel")),
        name="gdn_scan",
    )(p, p, p, p, conv_w, conv_w, conv_w, ad)


def _rope_kernel(x_ref, cos_ref, sa_ref, sb_ref, o_ref):
    scale = jnp.where(pl.program_id(1) == 0, DIFF_DH ** -0.5, 1.0)
    cos, sa, sb = cos_ref[...] * scale, sa_ref[...] * scale, sb_ref[...] * scale
    for hh in range(DIFF_HEADS):
        x = x_ref[:, hh * LANES:(hh + 1) * LANES]
        y = x * cos + pltpu.roll(x, LANES - 16, 1) * sa + pltpu.roll(x, 16, 1) * sb
        o_ref[:, hh * LANES:(hh + 1) * LANES] = y.astype(BF16)


def _rope(p, cos_t, sa_t, sb_t, *, tm):
    n = p.shape[0]
    tblocks = cos_t.shape[0] // tm
    wq = DIFF_HEADS * LANES
    tspec = pl.BlockSpec((tm, LANES), lambda i, j: (i % tblocks, 0))
    return pl.pallas_call(
        _rope_kernel,
        out_shape=jax.ShapeDtypeStruct((n, 2 * wq), BF16),
        grid=(n // tm, 2),
        in_specs=[pl.BlockSpec((tm, wq), lambda i, j: (i, C_Q // wq + j)), tspec, tspec, tspec],
        out_specs=pl.BlockSpec((tm, wq), lambda i, j: (i, j)),
        compiler_params=_cparams(("parallel", "parallel")),
        name="rope",
    )(p, cos_t, sa_t, sb_t)


def _rope_tables(t_len, l_ctx):
    pos = jnp.arange(t_len - l_ctx)
    row = (pos // GRID_W).astype(F32)
    colp = (pos % GRID_W).astype(F32)
    half = DIFF_DH // 2
    inv = 1.0 / (ROPE_THETA ** (jnp.arange(0, half, 2, dtype=F32) / half))
    lane = jnp.arange(LANES)
    part = (lane % DIFF_DH) // half
    upper = (lane % half) // (half // 2)
    ang = jnp.where(part[None, :] == 0, row[:, None], colp[:, None]) * inv[lane % (half // 2)][None, :]
    cos, sin = jnp.cos(ang), jnp.sin(ang)
    sa = jnp.where(upper[None, :] == 0, -sin, 0.0)
    sb = jnp.where(upper[None, :] == 1, sin, 0.0)
    ident = lambda v, fill: jnp.concatenate([jnp.full((l_ctx, LANES), fill, F32), v.astype(F32)], axis=0)
    return ident(cos, 1.0), ident(sa, 0.0), ident(sb, 0.0)


def _attn_kernel(q_ref, k_ref, v_ref, lam_ref, nw_ref, o_ref, *, l_ctx, lambda_init):
    tq = q_ref.shape[0]
    t_len = k_ref.shape[0]
    lv = lam_ref[...]
    lam = (jnp.exp(jnp.sum(lv[0:1] * lv[1:2], axis=-1, keepdims=True))
           - jnp.exp(jnp.sum(lv[2:3] * lv[3:4], axis=-1, keepdims=True)) + lambda_init)
    lane = lax.broadcasted_iota(jnp.int32, (1, LANES), 1)

    def attend(kv_len):
        q = q_ref[...]
        k = k_ref[0:kv_len, :]
        v = v_ref[0:kv_len, :].astype(BF16)
        es, inv = [], []
        for m in range(2):
            qm = jnp.where((lane >= DIFF_DH) == (m == 1), q, jnp.zeros_like(q))
            s = _dot_nt(qm, k)
            e = jnp.exp(s - jnp.max(s, axis=-1, keepdims=True))
            es.append(e)
            inv.append(1.0 / jnp.sum(e, axis=-1, keepdims=True))
        w = es[0] * inv[0] - es[1] * (lam * inv[1])
        o = _dot(w.astype(BF16), v)
        o_ref[...] = _rms(o, DIFF_EPS) * nw_ref[...] * (1.0 - lambda_init)

    is_ctx = pl.program_id(2) < l_ctx // tq
    pl.when(is_ctx)(lambda: attend(l_ctx))
    pl.when(jnp.logical_not(is_ctx))(lambda: attend(t_len))


def _diff_attn(qk, p, lam_vecs, norm_w, *, n_batch, t_len, l_ctx, lambda_init):
    n = p.shape[0]
    tq = min(l_ctx, 256)
    nt = t_len // tq
    kern = functools.partial(_attn_kernel, l_ctx=l_ctx, lambda_init=lambda_init)
    return pl.pallas_call(
        kern,
        out_shape=jax.ShapeDtypeStruct((n, DIFF_HEADS * DIFF_DV), F32),
        grid=(n_batch, DIFF_HEADS, nt),
        in_specs=[pl.BlockSpec((tq, LANES), lambda b, h, t: (b * nt + t, h)),
                  pl.BlockSpec((t_len, LANES), lambda b, h, t: (b, DIFF_HEADS + h)),
                  pl.BlockSpec((t_len, LANES), lambda b, h, t: (b, C_V // LANES + h)),
                  pl.BlockSpec((4, DIFF_DH), lambda b, h, t: (0, 0)),
                  pl.BlockSpec((1, DIFF_DV), lambda b, h, t: (0, 0))],
        out_specs=pl.BlockSpec((tq, DIFF_DV), lambda b, h, t: (b * nt + t, h)),
        compiler_params=_cparams(("parallel", "parallel", "parallel")),
        name="diff_attn",
    )(qk, qk, p, lam_vecs, norm_w.reshape(1, DIFF_DV))


def _out_proj_kernel(oa_ref, oga_ref, ob_ref, ogb_ref, yd_ref, anw_ref, bnw_ref, w_ref, x_ref, mod_ref,
                     n2_ref, rw_ref, rb_ref, xo_ref, h2_ref, te_ref, tg_ref, *, subs_per_batch, n_batch):
    m = mod_ref[_group_of(pl.program_id(0), subs_per_batch, n_batch)]
    pieces = []
    for o_ref, og_ref, nw_ref, heads in ((oa_ref, oga_ref, anw_ref, GLA_HEADS), (ob_ref, ogb_ref, bnw_ref, GDN_HEADS)):
        for hh in range(heads):
            sl = slice(hh * LANES, (hh + 1) * LANES)
            pieces.append((_rms(o_ref[:, sl], NORM_EPS) * nw_ref[...] * _silu(og_ref[:, sl])).astype(BF16))
    pieces.append(yd_ref[...].astype(BF16))
    mix = jnp.concatenate(pieces, axis=1)
    x_new = x_ref[...] + m[2:3] * _dot(mix, w_ref[...])
    xo_ref[...] = x_new
    h2 = _rms(x_new, NORM_EPS) * n2_ref[...] * (1.0 + m[4:5]) + m[3:4]
    h2_ref[...] = h2
    lg = _dot3(h2, rw_ref[...]) + rb_ref[...]
    lane = lax.broadcasted_iota(jnp.int32, lg.shape, 1)
    lane_f = lane.astype(F32)
    vals, idxs = [], []
    for _ in range(TOP_K):
        mx = jnp.max(lg, axis=-1, keepdims=True)
        ix = jnp.min(jnp.where(lg == mx, lane_f, float(LANES)), axis=-1, keepdims=True)
        vals.append(mx)
        idxs.append(ix)
        lg = jnp.where(lane_f == ix, -jnp.inf, lg)
    ex = [jnp.exp(v - vals[0]) for v in vals]
    inv_den = 1.0 / (ex[0] + ex[1] + ex[2] + ex[3])
    te = jnp.zeros(lg.shape, F32)
    tg = jnp.zeros(lg.shape, F32)
    for kk in range(TOP_K):
        te = jnp.where(lane == kk, idxs[kk], te)
        tg = jnp.where(lane == kk, ex[kk] * inv_den, tg)
    te_ref[...] = te.astype(jnp.int32)
    tg_ref[...] = tg


def _out_proj(oa, ob, yd, p, gla_nw, gdn_nw, w_out, xt, mod, norm2_w, rw_pad, rb_pad, *, tm, subs_per_batch, n_batch):
    n, d = xt.shape
    kern = functools.partial(_out_proj_kernel, subs_per_batch=subs_per_batch, n_batch=n_batch)
    const = lambda shape: pl.BlockSpec(shape, lambda i: (0,) * len(shape))
    wa, wb, wd = GLA_HEADS * GLA_DV, GDN_HEADS * GDN_DV, DIFF_HEADS * DIFF_DV
    return pl.pallas_call(
        kern,
        out_shape=(jax.ShapeDtypeStruct((n, d), F32), jax.ShapeDtypeStruct((n, d), F32),
                   jax.ShapeDtypeStruct((n, LANES), jnp.int32), jax.ShapeDtypeStruct((n, LANES), F32)),
        grid=(n // tm,),
        in_specs=[pl.BlockSpec((tm, wa), lambda i: (i, 0)),
                  pl.BlockSpec((tm, wa), lambda i: (i, A_OG // wa)),
                  pl.BlockSpec((tm, wb), lambda i: (i, 0)),
                  pl.BlockSpec((tm, wb), lambda i: (i, B_OG // wb)),
                  pl.BlockSpec((tm, wd), lambda i: (i, 0)),
                  const((1, LANES)), const((1, LANES)), const((D_MIX, d)),
                  pl.BlockSpec((tm, d), lambda i: (i, 0)),
                  const((MOD_ROWS, 6, d)), const((1, d)), const((d, LANES)), const((1, LANES))],
        out_specs=(pl.BlockSpec((tm, d), lambda i: (i, 0)), pl.BlockSpec((tm, d), lambda i: (i, 0)),
                   pl.BlockSpec((tm, LANES), lambda i: (i, 0)), pl.BlockSpec((tm, LANES), lambda i: (i, 0))),
        compiler_params=_cparams(("parallel",)),
        name="out_proj",
    )(oa, p, ob, p, yd, gla_nw.reshape(1, LANES), gdn_nw.reshape(1, LANES), w_out, xt, mod,
      norm2_w.reshape(1, d), rw_pad, rb_pad)


MOE_SUB = 256
PERM = 256


MOE_UNROLL = 4


def _for_sub_blocks(code, blocks):
    n = code // 2
    full = lambda first, count: blocks(
        [pl.ds(pl.multiple_of((first + u) * MOE_SUB, MOE_SUB), MOE_SUB) for u in range(count)])
    groups = n // MOE_UNROLL

    def body(i, carry):
        full(i * MOE_UNROLL, MOE_UNROLL)
        return carry

    lax.fori_loop(0, groups, body, 0)
    done = groups * MOE_UNROLL
    step = MOE_UNROLL // 2
    while step:
        take = ((n - done) & step) != 0
        pl.when(take)(functools.partial(full, done, step))
        done = done + jnp.where(take, step, 0)
        step //= 2
    pl.when(code % 2 == 1)(
        lambda: blocks([pl.ds(pl.multiple_of(n * MOE_SUB, MOE_SUB // 2), MOE_SUB // 2)]))


def _moe1_kernel(ve_ref, vn_ref, nu_ref, x_ref, w_ref, b_ref, perm_ref, o_ref, wp_scr):
    v = pl.program_id(0)
    tn = w_ref.shape[3]

    @pl.when(vn_ref[v] > 0)
    def _():
        for c in range(tn // PERM):
            wc = w_ref[0, 0, :, c * PERM:(c + 1) * PERM].astype(BF16)
            wp_scr[:, c * PERM:(c + 1) * PERM] = _dot(wc, perm_ref[...]).astype(BF16)

    def blocks(rows):
        hids = [_dot(x_ref[r, :].astype(BF16), wp_scr[...]) for r in rows]
        for r, hid in zip(rows, hids):
            hid = hid + b_ref[0]
            for c in range(tn // PERM):
                glu = jnp.minimum(hid[:, c * PERM:c * PERM + PERM // 2], SWIGLU_LIMIT)
                lin = jnp.clip(hid[:, c * PERM + PERM // 2:(c + 1) * PERM], -SWIGLU_LIMIT, SWIGLU_LIMIT)
                act = glu * jax.nn.sigmoid(SWIGLU_ALPHA * glu) * (lin + 1.0)
                o_ref[r, c * (PERM // 2):(c + 1) * (PERM // 2)] = act.astype(BF16)

    _for_sub_blocks(vn_ref[v], blocks)


def _moe2_kernel(ve_ref, vn_ref, nu_ref, a_ref, w_ref, b_ref, o_ref, wb_scr):
    v = pl.program_id(0)

    @pl.when(vn_ref[v] > 0)
    def _():
        wb_scr[...] = w_ref[0, 0].astype(BF16)

    def blocks(rows):
        outs = [_dot(a_ref[r, :], wb_scr[...]) for r in rows]
        for r, out in zip(rows, outs):
            o_ref[r, :] = out + b_ref[0]

    _for_sub_blocks(vn_ref[v], blocks)


def _moe_experts(xs, vis_e, vis_n, n_used, w1, b1p, w2, b2, perm, *, layer, rows_per_visit):
    _, n_exp, d, dh2 = w1.shape
    dh = dh2 // 2
    r = rows_per_visit
    n_vis = xs.shape[0] // r

    def maps(n_j):
        live = lambda v, nu: v < nu[0]
        jj = lambda v, j, nu: jnp.where(live(v, nu), j, n_j - 1)
        x_map = lambda v, j, ve, vn, nu: (jnp.minimum(v, nu[0] - 1), 0)
        w_map = lambda v, j, ve, vn, nu: (layer, ve[v], 0, jj(v, j, nu))
        b_map = lambda v, j, ve, vn, nu: (ve[v], 0, jj(v, j, nu))
        o_map = lambda v, j, ve, vn, nu: (jnp.where(live(v, nu), v, n_vis), jj(v, j, nu))
        return x_map, w_map, b_map, o_map

    tn1 = _pick(dh2, 512)
    x_map, w_map, b_map, o_map = maps(dh2 // tn1)
    act = pl.pallas_call(
        _moe1_kernel,
        out_shape=jax.ShapeDtypeStruct(((n_vis + 1) * r, dh), BF16),
        grid_spec=pltpu.PrefetchScalarGridSpec(
            num_scalar_prefetch=3, grid=(n_vis, dh2 // tn1),
            in_specs=[pl.BlockSpec((r, d), x_map),
                      pl.BlockSpec((1, 1, d, tn1), w_map),
                      pl.BlockSpec((1, 1, tn1), b_map),
                      pl.BlockSpec((PERM, PERM), lambda v, j, ve, vn, nu: (0, 0))],
            out_specs=pl.BlockSpec((r, tn1 // 2), o_map),
            scratch_shapes=[pltpu.VMEM((d, tn1), BF16)]),
        compiler_params=_cparams(("arbitrary", "arbitrary")),
        name="moe_up",
    )(vis_e, vis_n, n_used, xs, w1, b1p, perm)
    tn2 = _pick(d, 512)
    x_map, w_map, b_map, o_map = maps(d // tn2)
    return pl.pallas_call(
        _moe2_kernel,
        out_shape=jax.ShapeDtypeStruct(((n_vis + 1) * r, d), F32),
        grid_spec=pltpu.PrefetchScalarGridSpec(
            num_scalar_prefetch=3, grid=(n_vis, d // tn2),
            in_specs=[pl.BlockSpec((r, dh), x_map),
                      pl.BlockSpec((1, 1, dh, tn2), w_map),
                      pl.BlockSpec((1, 1, tn2), b_map)],
            out_specs=pl.BlockSpec((r, tn2), o_map),
            scratch_shapes=[pltpu.VMEM((dh, tn2), BF16)]),
        compiler_params=_cparams(("arbitrary", "arbitrary")),
        name="moe_down",
    )(vis_e, vis_n, n_used, act, w2, b2.reshape(n_exp, 1, d))


def _perm_matrix():
    src = jnp.arange(PERM)
    dst = jnp.where(src % 2 == 0, src // 2, PERM // 2 + src // 2)
    return (dst[:, None] == jnp.arange(PERM)[None, :]).astype(BF16)


def _deinterleave_bias(b1):
    n_exp, dh2 = b1.shape
    b = b1.reshape(n_exp, dh2 // PERM, PERM // 2, 2)
    return jnp.concatenate([b[..., 0], b[..., 1]], axis=-1).reshape(n_exp, 1, dh2)


def _routing(top_e, n_tok, rows_per_visit):
    r = rows_per_visit
    n_assign = n_tok * TOP_K
    n_vis = n_assign // r + N_EXPERTS
    flat_e = top_e.reshape(-1)
    onehot = (flat_e[:, None] == jnp.arange(N_EXPERTS)[None, :]).astype(jnp.int32)
    rank = jnp.take_along_axis(jnp.cumsum(onehot, axis=0) - onehot, flat_e[:, None], axis=1)[:, 0]
    counts = jnp.sum(onehot, axis=0)
    seg_vis = (counts + r - 1) // r
    vis_end = jnp.cumsum(seg_vis)
    vis_start = vis_end - seg_vis
    dest = vis_start[flat_e] * r + rank
    v = jnp.arange(n_vis)
    n_used = vis_end[-1]
    vc = jnp.minimum(v, n_used - 1)
    vis_e = jnp.minimum(jnp.searchsorted(vis_end, vc, side="right"), N_EXPERTS - 1).astype(jnp.int32)
    valid = jnp.clip(counts[vis_e] - (vc - vis_start[vis_e]) * r, 0, r)
    left = valid % MOE_SUB
    n_full = valid // MOE_SUB + (left > MOE_SUB // 2)
    half = (left > 0) & (left <= MOE_SUB // 2)
    vis_n = jnp.where(v < n_used, 2 * n_full + half, 0).astype(jnp.int32)
    tail = jnp.minimum(vis_start * r + counts // MOE_SUB * MOE_SUB, (n_vis * r) - MOE_SUB).astype(jnp.int32)
    return dest.astype(jnp.int32), tail, vis_e, vis_n, n_used.reshape(1).astype(jnp.int32), n_vis


def _dispatch_kernel(idx_ref, tail_ref, h_ref, xs_ref, zero_scr, zsem, sem):
    tm = h_ref.shape[0]

    def zero_copy(e):
        start = pl.multiple_of(tail_ref[0, e], MOE_SUB)
        return pltpu.make_async_copy(zero_scr, xs_ref.at[pl.ds(start, MOE_SUB), :], zsem)

    @pl.when(pl.program_id(0) == 0)
    def _():
        zero_scr[...] = jnp.zeros(zero_scr.shape, F32)
        for e in range(N_EXPERTS):
            zero_copy(e).start()
        for e in range(N_EXPERTS):
            zero_copy(e).wait()

    def row_copy(t, k, row):
        return pltpu.make_async_copy(h_ref.at[pl.ds(t, 1), :], xs_ref.at[pl.ds(row, 1), :], sem)

    def issue(t, carry):
        for k in range(TOP_K):
            row_copy(t, k, idx_ref[0, 0, t * TOP_K + k]).start()
        return carry

    def drain(t, carry):
        for k in range(TOP_K):
            row_copy(t, k, 0).wait()
        return carry

    lax.fori_loop(0, tm, issue, 0)
    lax.fori_loop(0, tm, drain, 0)


def _dispatch(h2, dest, tail, *, n_rows, sub):
    n, d = h2.shape
    tm = min(sub, 128)
    return pl.pallas_call(
        _dispatch_kernel,
        out_shape=jax.ShapeDtypeStruct((n_rows, d), F32),
        grid=(n // tm,),
        in_specs=[pl.BlockSpec((1, 1, tm * TOP_K), lambda i: (i, 0, 0), memory_space=pltpu.SMEM),
                  pl.BlockSpec(memory_space=pltpu.SMEM),
                  pl.BlockSpec((tm, d), lambda i: (i, 0))],
        out_specs=pl.BlockSpec(memory_space=pl.ANY),
        scratch_shapes=[pltpu.VMEM((MOE_SUB, d), F32), pltpu.SemaphoreType.DMA(()), pltpu.SemaphoreType.DMA(())],
        compiler_params=_cparams(("arbitrary",)),
        name="moe_dispatch",
    )(dest.reshape(n // tm, 1, tm * TOP_K), tail.reshape(1, N_EXPERTS), h2)


def _combine_kernel(idx_ref, x_ref, g_ref, mod_ref, fw_ref, yb_ref, o_ref, buf, sem, *,
                    sub, subs_per_batch, n_batch, final):
    tm = x_ref.shape[0]

    def row_copy(t, k, row):
        return pltpu.make_async_copy(yb_ref.at[pl.ds(row, 1), :], buf.at[k, pl.ds(t, 1), :], sem)

    def issue(t, carry):
        for k in range(TOP_K):
            row_copy(t, k, idx_ref[0, 0, t * TOP_K + k]).start()
        return carry

    def drain(t, carry):
        for k in range(TOP_K):
            row_copy(t, k, 0).wait()
        return carry

    lax.fori_loop(0, tm, issue, 0)
    lax.fori_loop(0, tm, drain, 0)
    g = g_ref[...]
    y = g[:, 0:1] * buf[0]
    for k in range(1, TOP_K):
        y = y + g[:, k:k + 1] * buf[k]
    m = mod_ref[_group_of((pl.program_id(0) * tm) // sub, subs_per_batch, n_batch)]
    x_new = x_ref[...] + m[5:6] * y
    o_ref[...] = _rms(x_new, NORM_EPS) * fw_ref[...] if final else x_new


def _combine(xt, yb, dest, top_g, mod, final_w, *, sub, subs_per_batch, n_batch, final):
    n, d = xt.shape
    tm = min(sub, 128)
    kern = functools.partial(_combine_kernel, sub=sub, subs_per_batch=subs_per_batch, n_batch=n_batch, final=final)
    row = pl.BlockSpec((tm, d), lambda i: (i, 0))
    return pl.pallas_call(
        kern,
        out_shape=jax.ShapeDtypeStruct((n, d), F32),
        grid=(n // tm,),
        in_specs=[pl.BlockSpec((1, 1, tm * TOP_K), lambda i: (i, 0, 0), memory_space=pltpu.SMEM),
                  row, pl.BlockSpec((tm, LANES), lambda i: (i, 0)),
                  pl.BlockSpec((MOD_ROWS, 6, d), lambda i: (0, 0, 0)), pl.BlockSpec((1, d), lambda i: (0, 0)),
                  pl.BlockSpec(memory_space=pl.ANY)],
        out_specs=row,
        scratch_shapes=[pltpu.VMEM((TOP_K, tm, d), F32), pltpu.SemaphoreType.DMA(())],
        compiler_params=_cparams(("arbitrary",)),
        name="moe_combine",
    )(dest.reshape(n // tm, 1, tm * TOP_K), xt, top_g, mod, final_w.reshape(1, d), yb)


def _pad_cols(w, cuts):
    parts = []
    for start, width, padded in cuts:
        parts.append(w[:, start:start + width])
        if padded > width:
            parts.append(jnp.zeros((w.shape[0], padded - width), w.dtype))
    return jnp.concatenate(parts, axis=1)


def kernel(x, c, ctx, c_ctx, w_mod, b_mod, norm1_w, w_in, gla_gate_w, gla_gate_b, gla_norm_w, gdn_conv_w,
           gdn_a_log, gdn_dt_bias, gdn_norm_w, diff_lambda, diff_norm_w, w_out, norm2_w, router_w, router_b,
           expert_w1, expert_b1, expert_w2, expert_b2, final_norm_w):
    n_batch, s_len, d = x.shape
    l_ctx = ctx.shape[1]
    depth = w_mod.shape[0]
    t_len = l_ctx + s_len
    n_tok = n_batch * t_len
    subs_per_batch = t_len // l_ctx
    assert s_len % l_ctx == 0 and l_ctx % CHUNK == 0 and n_batch < MOD_ROWS and l_ctx % 16 == 0
    rows_per_visit = -(-(n_tok * TOP_K // N_EXPERTS * 9 // 8) // MOE_SUB) * MOE_SUB

    xt = jnp.concatenate([ctx, x], axis=1).reshape(n_tok, d)
    cvec = jnp.concatenate([c, c_ctx[None, :], jnp.zeros((MOD_ROWS - n_batch - 1, d), F32)], axis=0)
    mod_all = _modulation(cvec, w_mod, b_mod).reshape(depth, MOD_ROWS, 6, d)
    cos_t, sa_t, sb_t = _rope_tables(t_len, l_ctx)
    perm = _perm_matrix()
    a_w = GLA_HEADS * (2 * GLA_DK + 2 * GLA_DV)
    b_w = GDN_HEADS * (2 * GDN_DK + 2 * GDN_DV)
    c_w = DIFF_HEADS * (4 * DIFF_DH + DIFF_DV)
    glr0, b0 = a_w, a_w + 2 * GLA_RANK
    ba0 = b0 + b_w
    c0 = ba0 + 4 * GDN_HEADS
    cuts = ((0, a_w, a_w), (b0, b_w, b_w), (c0, c_w, c_w), (glr0, 2 * GLA_RANK, LANES), (ba0, 4 * GDN_HEADS, LANES))
    assert (a_w, a_w + b_w, a_w + b_w + c_w) == (B_Q, C_Q, A_GLR)

    for l in range(depth):
        lambda_init = 0.8 - 0.6 * math.exp(-0.3 * l)
        mod = mod_all[l]
        w_pad = _pad_cols(w_in[l], cuts).astype(BF16)
        p = _in_proj(xt, norm1_w[l], mod, w_pad, sub=l_ctx, subs_per_batch=subs_per_batch, n_batch=n_batch)

        gw = gla_gate_w[l]
        gw_pad = jnp.zeros((2, LANES, GLA_HEADS * GLA_DK), F32)
        gw_pad = gw_pad.at[0, 0:GLA_RANK].set(gw[0]).at[1, GLA_RANK:2 * GLA_RANK].set(gw[1])
        oa = _gla(p, gw_pad, gla_gate_b[l].reshape(2, 1, -1), n_batch=n_batch, t_len=t_len, l_ctx=l_ctx)

        ad = jnp.zeros((8, LANES), F32)
        ad = ad.at[0, 2 * GDN_HEADS:4 * GDN_HEADS].set(gdn_a_log[l].reshape(-1))
        ad = ad.at[1, 2 * GDN_HEADS:4 * GDN_HEADS].set(gdn_dt_bias[l].reshape(-1))
        ob = _gdn(p, gdn_conv_w[l], ad, n_batch=n_batch, t_len=t_len, l_ctx=l_ctx)

        qk = _rope(p, cos_t, sa_t, sb_t, tm=l_ctx)
        yd = _diff_attn(qk, p, diff_lambda[l], diff_norm_w[l], n_batch=n_batch, t_len=t_len, l_ctx=l_ctx,
                        lambda_init=lambda_init)

        rw_pad = jnp.zeros((d, LANES), F32).at[:, :N_EXPERTS].set(router_w[l])
        rb_pad = jnp.full((1, LANES), -1e30, F32).at[0, :N_EXPERTS].set(router_b[l])
        xt, h2, top_e, top_g = _out_proj(oa, ob, yd, p, gla_norm_w[l], gdn_norm_w[l], w_out[l].astype(BF16), xt, mod,
                                         norm2_w[l], rw_pad, rb_pad, tm=l_ctx, subs_per_batch=subs_per_batch,
                                         n_batch=n_batch)

        dest, tail, vis_e, vis_n, n_used, n_vis = _routing(top_e[:, :TOP_K], n_tok, rows_per_visit)
        xs = _dispatch(h2, dest, tail, n_rows=n_vis * rows_per_visit, sub=l_ctx)
        yb = _moe_experts(xs, vis_e, vis_n, n_used, expert_w1, _deinterleave_bias(expert_b1[l]),
                          expert_w2, expert_b2[l], perm, layer=l, rows_per_visit=rows_per_visit)
        xt = _combine(xt, yb, dest, top_g, mod, final_norm_w, sub=l_ctx, subs_per_batch=subs_per_batch,
                      n_batch=n_batch, final=(l == depth - 1))
    return xt.reshape(n_batch, t_len, d)[:, l_ctx:, :]
```

```python
import functools
import math

import jax
import jax.numpy as jnp
from jax import lax
from jax.experimental import pallas as pl
from jax.experimental.pallas import tpu as pltpu

F32 = jnp.float32
BF16 = jnp.bfloat16

GRID_W = 64
GLA_HEADS, GLA_DK, GLA_DV, GLA_RANK, GLA_GATE_NORM = 4, 64, 128, 16, 16.0
GDN_HEADS, GDN_DK, GDN_DV, GDN_CONV = 6, 128, 128, 5
DIFF_HEADS, DIFF_DH, DIFF_DV, DIFF_EPS = 6, 64, 128, 1e-5
ROPE_THETA = 10000.0
CHUNK = 64
N_EXPERTS, TOP_K = 32, 4
SWIGLU_ALPHA, SWIGLU_LIMIT = 1.702, 7.0
NORM_EPS = 1e-6
LANES = 128
MXU_N = 256
MOD_ROWS = 8

A_Q, A_K, A_V, A_OG = 0, 256, 512, 1024
B_Q, B_K, B_V, B_OG = 1536, 2304, 3072, 3840
C_Q, C_K, C_V = 4608, 5376, 6144
A_GLR, B_BA = 6912, 7040
NP_IN = 7168
D_MIX = GLA_HEADS * GLA_DV + GDN_HEADS * GDN_DV + DIFF_HEADS * DIFF_DV

VMEM_LIMIT = 56 * 1024 * 1024

def _dot(a, b):
    return jnp.dot(a, b, preferred_element_type=F32)


def _dot_nt(a, b):
    return lax.dot_general(a, b, (((1,), (1,)), ((), ())), preferred_element_type=F32)


def _split3(x):
    hi = x.astype(BF16)
    r = x - hi.astype(F32)
    mid = r.astype(BF16)
    return hi, mid, (r - mid.astype(F32)).astype(BF16)


def _dot_sel(sel, x, nt=False):
    dot = _dot_nt if nt else _dot
    s = sel.astype(F32).astype(BF16)
    hi, mid, lo = _split3(x)
    return dot(s, hi) + dot(s, mid) + dot(s, lo)


def _dot3(a, b):
    a_hi = a.astype(BF16)
    b_hi = b.astype(BF16)
    a_lo = (a - a_hi.astype(F32)).astype(BF16)
    b_lo = (b - b_hi.astype(F32)).astype(BF16)
    return _dot(a_hi, b_hi) + _dot(a_hi, b_lo) + _dot(a_lo, b_hi)


def _pick(n, pref):
    if n <= pref:
        return n
    t = pref - pref % LANES
    while t > LANES and n % t:
        t -= LANES
    return t


def _silu(x):
    return x * jax.nn.sigmoid(x)


def _log_sigmoid(x):
    return jnp.minimum(x, 0.0) - jnp.log(1.0 + jnp.exp(-jnp.abs(x)))


def _softplus(x):
    return jnp.maximum(x, 0.0) + jnp.log(1.0 + jnp.exp(-jnp.abs(x)))


def _cparams(sem):
    return pltpu.CompilerParams(dimension_semantics=sem, vmem_limit_bytes=VMEM_LIMIT)


def _mod_kernel(c_ref, w_ref, b_ref, o_ref):
    a = _silu(c_ref[...])
    o_ref[0] = _dot(a.astype(BF16), w_ref[0].astype(BF16)) + b_ref[0]


def _modulation(cvec, w_mod, b_mod):
    depth, d, n6 = w_mod.shape
    tn = _pick(n6, 1024)
    return pl.pallas_call(
        _mod_kernel,
        out_shape=jax.ShapeDtypeStruct((depth, MOD_ROWS, n6), F32),
        grid=(depth, n6 // tn),
        in_specs=[pl.BlockSpec((MOD_ROWS, d), lambda l, j: (0, 0)),
                  pl.BlockSpec((1, d, tn), lambda l, j: (l, 0, j)),
                  pl.BlockSpec((1, 1, tn), lambda l, j: (l, 0, j))],
        out_specs=pl.BlockSpec((1, MOD_ROWS, tn), lambda l, j: (l, 0, j)),
        compiler_params=_cparams(("parallel", "parallel")),
        name="modulation",
    )(cvec, w_mod, b_mod.reshape(depth, 1, n6))


def _group_of(sub_block, subs_per_batch, n_batch):
    return jnp.where(sub_block % subs_per_batch == 0, n_batch, sub_block // subs_per_batch)


def _rms(x, eps):
    return x * lax.rsqrt(jnp.mean(x * x, axis=-1, keepdims=True) + eps)


def _in_proj_kernel(x_ref, nw_ref, mod_ref, w_ref, o_ref, h_scr, *, sub, subs_per_batch, n_batch):
    i = pl.program_id(0)

    @pl.when(pl.program_id(1) == 0)
    def _():
        for s in range(x_ref.shape[0] // sub):
            m = mod_ref[_group_of(i * (x_ref.shape[0] // sub) + s, subs_per_batch, n_batch)]
            y = _rms(x_ref[s * sub:(s + 1) * sub, :], NORM_EPS) * nw_ref[...]
            h_scr[s * sub:(s + 1) * sub, :] = (y * (1.0 + m[1:2]) + m[0:1]).astype(BF16)

    o_ref[...] = _dot(h_scr[...], w_ref[...])


def _in_proj(xt, norm_w, mod, w_pad, *, sub, subs_per_batch, n_batch):
    n, d = xt.shape
    npad = w_pad.shape[1]
    tm = sub * 2 if (n // sub) % 2 == 0 else sub
    tn = _pick(npad, 1024)
    kern = functools.partial(_in_proj_kernel, sub=sub, subs_per_batch=subs_per_batch, n_batch=n_batch)
    return pl.pallas_call(
        kern,
        out_shape=jax.ShapeDtypeStruct((n, npad), F32),
        grid=(n // tm, npad // tn),
        in_specs=[pl.BlockSpec((tm, d), lambda i, j: (i, 0)),
                  pl.BlockSpec((1, d), lambda i, j: (0, 0)),
                  pl.BlockSpec((MOD_ROWS, 6, d), lambda i, j: (0, 0, 0)),
                  pl.BlockSpec((d, tn), lambda i, j: (0, j))],
        out_specs=pl.BlockSpec((tm, tn), lambda i, j: (i, j)),
        scratch_shapes=[pltpu.VMEM((tm, d), BF16)],
        compiler_params=_cparams(("parallel", "arbitrary")),
        name="in_proj",
    )(xt, norm_w.reshape(1, d), mod, w_pad)


def _tri_masks(c):
    r = lax.broadcasted_iota(jnp.int32, (c, c), 0)
    s = lax.broadcasted_iota(jnp.int32, (c, c), 1)
    return ((s <= r, s < r), (s >= r, s > r))


def _bwd_chunk(i, nc_ctx, nc):
    return jnp.where(i < nc_ctx, nc_ctx - 1 - i, nc - 1 - (i - nc_ctx))


def _gla_kernel(q_ref, k_ref, v_ref, glr_ref, gw_ref, gb_ref, o_ref, st_scr, *, nc_ctx):
    t_len = q_ref.shape[0]
    nc = t_len // CHUNK
    masks = _tri_masks(CHUNK)
    lane = lax.broadcasted_iota(jnp.int32, (1, LANES), 1)
    head_mask = [(lane < GLA_DK).astype(F32), (lane >= GLA_DK).astype(F32)]
    o_ref[...] = jnp.zeros(o_ref.shape, F32)
    st_scr[...] = jnp.zeros(st_scr.shape, F32)

    def step(i, carry):
        cs = (i, _bwd_chunk(i, nc_ctx, nc))
        rows = [pl.ds(pl.multiple_of(c * CHUNK, CHUNK), CHUNK) for c in cs]
        pre = [_dot3(glr_ref[rows[z], :], gw_ref[z]) + gb_ref[z] for z in range(2)]
        b = [_dot_sel(masks[z][0], _log_sigmoid(pre[z]) / GLA_GATE_NORM) for z in range(2)]
        tot = [b[0][CHUNK - 1:CHUNK, :], b[1][0:1, :]]
        qd, kk, kd = [], [], []
        for z in range(2):
            k = k_ref[rows[z], :]
            qd.append(q_ref[rows[z], :] * GLA_DK ** -0.5 * jnp.exp(b[z]))
            kk.append((k * jnp.exp(-b[z])).astype(BF16))
            kd.append(k * jnp.exp(tot[z] - b[z]))
        zj = [(z, j) for z in range(2) for j in range(2)]
        qdj = [(qd[z] * head_mask[j]).astype(BF16) for z, j in zj]
        vj = [v_ref[rows[z], j * GLA_DV:(j + 1) * GLA_DV] for z, j in zj]
        st = [st_scr[z, j] for z, j in zj]
        att = [jnp.where(masks[z][0], _dot_nt(qdj[n], kk[z]), 0.0).astype(BF16) for n, (z, j) in enumerate(zj)]
        inter = [_dot_nt(qdj[n], st[n].astype(BF16)) for n in range(4)]
        intra = [_dot(att[n], vj[n].astype(BF16)) for n in range(4)]
        upd = [_dot(vj[n].T.astype(BF16), (kd[z] * head_mask[j]).astype(BF16)) for n, (z, j) in enumerate(zj)]
        for n, (z, j) in enumerate(zj):
            o_ref[rows[z], j * GLA_DV:(j + 1) * GLA_DV] += intra[n] + inter[n]
            st_scr[z, j] = st[n] * jnp.exp(tot[z]) + upd[n]
        return carry

    lax.fori_loop(0, nc, step, 0)


def _gla(p, gw_pad, gb, *, n_batch, t_len, l_ctx):
    n = p.shape[0]
    spec = lambda col, w: pl.BlockSpec((t_len, w), lambda b, hp, col=col, w=w: (b, col // w + hp))
    kern = functools.partial(_gla_kernel, nc_ctx=l_ctx // CHUNK)
    return pl.pallas_call(
        kern,
        out_shape=jax.ShapeDtypeStruct((n, GLA_HEADS * GLA_DV), F32),
        grid=(n_batch, GLA_HEADS // 2),
        in_specs=[spec(A_Q, LANES), spec(A_K, LANES), spec(A_V, 2 * GLA_DV),
                  pl.BlockSpec((t_len, LANES), lambda b, hp: (b, A_GLR // LANES)),
                  pl.BlockSpec((2, LANES, LANES), lambda b, hp: (0, 0, hp)),
                  pl.BlockSpec((2, 1, LANES), lambda b, hp: (0, 0, hp))],
        out_specs=pl.BlockSpec((t_len, 2 * GLA_DV), lambda b, hp: (b, hp)),
        scratch_shapes=[pltpu.VMEM((2, 2, GLA_DV, LANES), F32)],
        compiler_params=_cparams(("parallel", "parallel")),
        name="gla_scan",
    )(p, p, p, p, gw_pad, gb)


def _gdn_kernel(q_ref, k_ref, v_ref, ba_ref, cwq_ref, cwk_ref, cwv_ref, ad_ref, o_ref,
                m_scr, b_scr, qp_scr, dec_scr, s_scr, *, nc_ctx, l_ctx, group):
    t_len = q_ref.shape[0]
    nc = t_len // CHUNK
    h = pl.program_id(1)
    masks = _tri_masks(CHUNK)
    lane = lax.broadcasted_iota(jnp.int32, (1, LANES), 1)
    halo = 8
    pad = (GDN_CONV - 1) // 2
    ones_l0 = jnp.broadcast_to((lane == 0).astype(F32), (CHUNK, LANES))

    def conv(src_ref, cw_ref, r0):
        top = src_ref[pl.ds(pl.multiple_of(jnp.maximum(r0 - halo, 0), halo), halo), :]
        bot = src_ref[pl.ds(pl.multiple_of(jnp.minimum(r0 + CHUNK, t_len - halo), halo), halo), :]
        win = jnp.concatenate([top, src_ref[pl.ds(r0, CHUNK), :], bot], axis=0)
        row = r0 + lax.broadcasted_iota(jnp.int32, (CHUNK, 1), 0)
        acc = jnp.zeros((CHUNK, LANES), F32)
        for j in range(GDN_CONV):
            d = j - pad
            ok = (((row + d) >= l_ctx) == (row >= l_ctx)) & (row + d >= 0) & (row + d < t_len)
            acc = acc + jnp.where(ok, win[halo + d:halo + d + CHUNK, :], 0.0) * cw_ref[j:j + 1, :]
        return _silu(acc)

    def l2n(y):
        return y * lax.rsqrt(jnp.sum(y * y, axis=-1, keepdims=True) + 1e-6)

    def prep(gi, carry):
        per_chunk, chains = [], []
        for cc in range(group):
            c = gi * group + cc
            r0 = pl.multiple_of(c * CHUNK, CHUNK)
            q = l2n(conv(q_ref, cwq_ref, r0)) * GDN_DK ** -0.5
            k = l2n(conv(k_ref, cwk_ref, r0))
            v = conv(v_ref, cwv_ref, r0)
            ba = ba_ref[pl.ds(r0, CHUNK), :]
            beta_all = jax.nn.sigmoid(ba)
            g_all = -jnp.exp(ad_ref[0:1, :]) * _softplus(ba + ad_ref[1:2, :])
            per_chunk.append((c, r0, q, k, v, beta_all, g_all))
        grams = []
        for (_, _, q, k, _, _, _) in per_chunk:
            kbf = k.astype(BF16)
            grams.append((_dot_nt(kbf, kbf), _dot_nt(q.astype(BF16), kbf)))
        gcs = []
        for (_, _, _, _, _, beta_all, g_all) in per_chunk:
            for z in range(2):
                col = z * GDN_HEADS + h
                beta = jnp.sum(jnp.where(lane == col, beta_all, 0.0), axis=-1, keepdims=True)
                g = jnp.sum(jnp.where(lane == 2 * GDN_HEADS + col, g_all, 0.0), axis=-1, keepdims=True)
                gcs.append((beta, _dot_sel(masks[z][0], jnp.broadcast_to(g, (CHUNK, LANES)))))
        grows = [_dot_sel(ones_l0, gc, nt=True) for (_, gc) in gcs]
        xs, npows, aqk, qg, kdt, where = [], [], [], [], [], []
        for ci, (c, r0, q, k, v, _, _) in enumerate(per_chunk):
            kk, qk = grams[ci]
            for z in range(2):
                incl, strict = masks[z]
                beta, gc = gcs[2 * ci + z]
                dmat = jnp.exp(jnp.where(incl, gc[:, :CHUNK] - grows[2 * ci + z], -jnp.inf))
                tot = gc[CHUNK - 1:CHUNK, :] if z == 0 else gc[0:1, :]
                where.append((z, c, r0))
                aqk.append(jnp.where(incl, qk * dmat, 0.0).astype(BF16))
                qg.append(q * jnp.exp(gc))
                kdt.append((k * jnp.exp(tot - gc)).T.astype(BF16))
                dec_scr[z, pl.ds(pl.multiple_of(c * halo, halo), halo), :] = jnp.broadcast_to(jnp.exp(tot), (halo, LANES))
                npows.append((-jnp.where(strict, beta * kk * dmat, 0.0)).astype(BF16))
                xs.append(jnp.concatenate([v * beta, k * (beta * jnp.exp(gc))], axis=1))
        for lvl in range(6):
            xs = [x + _dot(npw, x.astype(BF16)) for x, npw in zip(xs, npows)]
            if lvl < 5:
                npows = [_dot(npw, npw).astype(BF16) for npw in npows]
        ub = [x[:, :GDN_DV].astype(BF16) for x in xs]
        wb = [x[:, GDN_DV:].astype(BF16) for x in xs]
        m_all = [_dot(kdt[n], wb[n]) for n in range(len(xs))]
        b_all = [_dot(kdt[n], ub[n]) for n in range(len(xs))]
        aw = [_dot(aqk[n], wb[n]) for n in range(len(xs))]
        au = [_dot(aqk[n], ub[n]) for n in range(len(xs))]
        for n, (z, c, r0) in enumerate(where):
            blk = pl.ds(pl.multiple_of(c * GDN_DK, GDN_DK), GDN_DK)
            m_scr[z, blk, :] = (-m_all[n]).astype(BF16)
            b_scr[z, blk, :] = b_all[n]
            qp_scr[z, pl.ds(r0, CHUNK), :] = (qg[n] - aw[n]).astype(BF16)
            o_ref[pl.ds(r0, CHUNK), :] += au[n]
        return carry

    o_ref[...] = jnp.zeros(o_ref.shape, F32)
    lax.fori_loop(0, nc // group, prep, 0)
    s_scr[...] = jnp.zeros(s_scr.shape, F32)

    def step(i, carry):
        cs = (i, _bwd_chunk(i, nc_ctx, nc))
        s = [s_scr[z] for z in range(2)]
        sb = [s[z].astype(BF16) for z in range(2)]
        blk = [pl.ds(pl.multiple_of(cs[z] * GDN_DK, GDN_DK), GDN_DK) for z in range(2)]
        rows = [pl.ds(pl.multiple_of(cs[z] * CHUNK, CHUNK), CHUNK) for z in range(2)]
        ms = [_dot(m_scr[z, blk[z], :], sb[z]) for z in range(2)]
        qs = [_dot(qp_scr[z, rows[z], :], sb[z]) for z in range(2)]
        for z in range(2):
            dec = dec_scr[z, pl.ds(pl.multiple_of(cs[z] * halo, halo), 1), :]
            s_scr[z] = s[z] * dec + ms[z] + b_scr[z, blk[z], :]
            o_ref[rows[z], :] += qs[z]
        return carry

    lax.fori_loop(0, nc, step, 0)


def _gdn(p, conv_w, ad, *, n_batch, t_len, l_ctx):
    n = p.shape[0]
    spec = lambda col: pl.BlockSpec((t_len, LANES), lambda b, h, col=col: (b, col // LANES + h))
    cspec = lambda off: pl.BlockSpec((GDN_CONV, LANES), lambda b, h, off=off: (0, off + h))
    nc = t_len // CHUNK
    group = next(g for g in (4, 2, 1) if nc % g == 0)
    kern = functools.partial(_gdn_kernel, nc_ctx=l_ctx // CHUNK, l_ctx=l_ctx, group=group)
    return pl.pallas_call(
        kern,
        out_shape=jax.ShapeDtypeStruct((n, GDN_HEADS * GDN_DV), F32),
        grid=(n_batch, GDN_HEADS),
        in_specs=[spec(B_Q), spec(B_K), spec(B_V),
                  pl.BlockSpec((t_len, LANES), lambda b, h: (b, B_BA // LANES)),
                  cspec(0), cspec(GDN_HEADS), cspec(2 * GDN_HEADS),
                  pl.BlockSpec((8, LANES), lambda b, h: (0, 0))],
        out_specs=pl.BlockSpec((t_len, GDN_DV), lambda b, h: (b, h)),
        scratch_shapes=[pltpu.VMEM((2, nc * GDN_DK, GDN_DK), BF16),
                        pltpu.VMEM((2, nc * GDN_DK, GDN_DV), F32),
                        pltpu.VMEM((2, t_len, GDN_DK), BF16),
                        pltpu.VMEM((2, nc * 8, LANES), F32),
                        pltpu.VMEM((2, GDN_DK, GDN_DV), F32)],
        compiler_params=_cparams(("parallel", "parallel")),
        name="gdn_scan",
    )(p, p, p, p, conv_w, conv_w, conv_w, ad)


def _rope_kernel(x_ref, cos_ref, sa_ref, sb_ref, o_ref):
    scale = jnp.where(pl.program_id(1) == 0, DIFF_DH ** -0.5, 1.0)
    cos, sa, sb = cos_ref[...] * scale, sa_ref[...] * scale, sb_ref[...] * scale
    for hh in range(DIFF_HEADS):
        x = x_ref[:, hh * LANES:(hh + 1) * LANES]
        y = x * cos + pltpu.roll(x, LANES - 16, 1) * sa + pltpu.roll(x, 16, 1) * sb
        o_ref[:, hh * LANES:(hh + 1) * LANES] = y.astype(BF16)


def _rope(p, cos_t, sa_t, sb_t, *, tm):
    n = p.shape[0]
    tblocks = cos_t.shape[0] // tm
    wq = DIFF_HEADS * LANES
    tspec = pl.BlockSpec((tm, LANES), lambda i, j: (i % tblocks, 0))
    return pl.pallas_call(
        _rope_kernel,
        out_shape=jax.ShapeDtypeStruct((n, 2 * wq), BF16),
        grid=(n // tm, 2),
        in_specs=[pl.BlockSpec((tm, wq), lambda i, j: (i, C_Q // wq + j)), tspec, tspec, tspec],
        out_specs=pl.BlockSpec((tm, wq), lambda i, j: (i, j)),
        compiler_params=_cparams(("parallel", "parallel")),
        name="rope",
    )(p, cos_t, sa_t, sb_t)


def _rope_tables(t_len, l_ctx):
    pos = jnp.arange(t_len - l_ctx)
    row = (pos // GRID_W).astype(F32)
    colp = (pos % GRID_W).astype(F32)
    half = DIFF_DH // 2
    inv = 1.0 / (ROPE_THETA ** (jnp.arange(0, half, 2, dtype=F32) / half))
    lane = jnp.arange(LANES)
    part = (lane % DIFF_DH) // half
    upper = (lane % half) // (half // 2)
    ang = jnp.where(part[None, :] == 0, row[:, None], colp[:, None]) * inv[lane % (half // 2)][None, :]
    cos, sin = jnp.cos(ang), jnp.sin(ang)
    sa = jnp.where(upper[None, :] == 0, -sin, 0.0)
    sb = jnp.where(upper[None, :] == 1, sin, 0.0)
    ident = lambda v, fill: jnp.concatenate([jnp.full((l_ctx, LANES), fill, F32), v.astype(F32)], axis=0)
    return ident(cos, 1.0), ident(sa, 0.0), ident(sb, 0.0)


def _attn_kernel(q_ref, k_ref, v_ref, lam_ref, nw_ref, o_ref, *, l_ctx, lambda_init):
    tq = q_ref.shape[0]
    t_len = k_ref.shape[0]
    lv = lam_ref[...]
    lam = (jnp.exp(jnp.sum(lv[0:1] * lv[1:2], axis=-1, keepdims=True))
           - jnp.exp(jnp.sum(lv[2:3] * lv[3:4], axis=-1, keepdims=True)) + lambda_init)
    lane = lax.broadcasted_iota(jnp.int32, (1, LANES), 1)

    def attend(kv_len):
        q = q_ref[...]
        k = k_ref[0:kv_len, :]
        v = v_ref[0:kv_len, :].astype(BF16)
        es, inv = [], []
        for m in range(2):
            qm = jnp.where((lane >= DIFF_DH) == (m == 1), q, jnp.zeros_like(q))
            s = _dot_nt(qm, k)
            e = jnp.exp(s - jnp.max(s, axis=-1, keepdims=True))
            es.append(e)
            inv.append(1.0 / jnp.sum(e, axis=-1, keepdims=True))
        w = es[0] * inv[0] - es[1] * (lam * inv[1])
        o = _dot(w.astype(BF16), v)
        o_ref[...] = _rms(o, DIFF_EPS) * nw_ref[...] * (1.0 - lambda_init)

    is_ctx = pl.program_id(2) < l_ctx // tq
    pl.when(is_ctx)(lambda: attend(l_ctx))
    pl.when(jnp.logical_not(is_ctx))(lambda: attend(t_len))


def _diff_attn(qk, p, lam_vecs, norm_w, *, n_batch, t_len, l_ctx, lambda_init):
    n = p.shape[0]
    tq = min(l_ctx, 256)
    nt = t_len // tq
    kern = functools.partial(_attn_kernel, l_ctx=l_ctx, lambda_init=lambda_init)
    return pl.pallas_call(
        kern,
        out_shape=jax.ShapeDtypeStruct((n, DIFF_HEADS * DIFF_DV), F32),
        grid=(n_batch, DIFF_HEADS, nt),
        in_specs=[pl.BlockSpec((tq, LANES), lambda b, h, t: (b * nt + t, h)),
                  pl.BlockSpec((t_len, LANES), lambda b, h, t: (b, DIFF_HEADS + h)),
                  pl.BlockSpec((t_len, LANES), lambda b, h, t: (b, C_V // LANES + h)),
                  pl.BlockSpec((4, DIFF_DH), lambda b, h, t: (0, 0)),
                  pl.BlockSpec((1, DIFF_DV), lambda b, h, t: (0, 0))],
        out_specs=pl.BlockSpec((tq, DIFF_DV), lambda b, h, t: (b * nt + t, h)),
        compiler_params=_cparams(("parallel", "parallel", "parallel")),
        name="diff_attn",
    )(qk, qk, p, lam_vecs, norm_w.reshape(1, DIFF_DV))


def _out_proj_kernel(oa_ref, oga_ref, ob_ref, ogb_ref, yd_ref, anw_ref, bnw_ref, w_ref, x_ref, mod_ref,
                     n2_ref, rw_ref, rb_ref, xo_ref, h2_ref, te_ref, tg_ref, *, subs_per_batch, n_batch):
    m = mod_ref[_group_of(pl.program_id(0), subs_per_batch, n_batch)]
    pieces = []
    for o_ref, og_ref, nw_ref, heads in ((oa_ref, oga_ref, anw_ref, GLA_HEADS), (ob_ref, ogb_ref, bnw_ref, GDN_HEADS)):
        for hh in range(heads):
            sl = slice(hh * LANES, (hh + 1) * LANES)
            pieces.append((_rms(o_ref[:, sl], NORM_EPS) * nw_ref[...] * _silu(og_ref[:, sl])).astype(BF16))
    pieces.append(yd_ref[...].astype(BF16))
    mix = jnp.concatenate(pieces, axis=1)
    x_new = x_ref[...] + m[2:3] * _dot(mix, w_ref[...])
    xo_ref[...] = x_new
    h2 = _rms(x_new, NORM_EPS) * n2_ref[...] * (1.0 + m[4:5]) + m[3:4]
    bits = lax.bitcast_convert_type(h2.astype(BF16).astype(F32), jnp.uint32)
    half = bits.shape[1] // 2
    h2_ref[...] = (bits[:, :half] >> 16) | (bits[:, half:] & jnp.uint32(0xFFFF0000))
    lg = _dot3(h2, rw_ref[...]) + rb_ref[...]
    lane = lax.broadcasted_iota(jnp.int32, lg.shape, 1)
    lane_f = lane.astype(F32)
    vals, idxs = [], []
    for _ in range(TOP_K):
        mx = jnp.max(lg, axis=-1, keepdims=True)
        ix = jnp.min(jnp.where(lg == mx, lane_f, float(LANES)), axis=-1, keepdims=True)
        vals.append(mx)
        idxs.append(ix)
        lg = jnp.where(lane_f == ix, -jnp.inf, lg)
    ex = [jnp.exp(v - vals[0]) for v in vals]
    inv_den = 1.0 / (ex[0] + ex[1] + ex[2] + ex[3])
    te = jnp.zeros(lg.shape, F32)
    tg = jnp.zeros(lg.shape, F32)
    for kk in range(TOP_K):
        te = jnp.where(lane == kk, idxs[kk], te)
        tg = jnp.where(lane == kk, ex[kk] * inv_den, tg)
    te_ref[...] = te.astype(jnp.int32)
    tg_ref[...] = tg


def _out_proj(oa, ob, yd, p, gla_nw, gdn_nw, w_out, xt, mod, norm2_w, rw_pad, rb_pad, *, tm, subs_per_batch, n_batch):
    n, d = xt.shape
    kern = functools.partial(_out_proj_kernel, subs_per_batch=subs_per_batch, n_batch=n_batch)
    const = lambda shape: pl.BlockSpec(shape, lambda i: (0,) * len(shape))
    wa, wb, wd = GLA_HEADS * GLA_DV, GDN_HEADS * GDN_DV, DIFF_HEADS * DIFF_DV
    return pl.pallas_call(
        kern,
        out_shape=(jax.ShapeDtypeStruct((n, d), F32), jax.ShapeDtypeStruct((n, d // 2), jnp.uint32),
                   jax.ShapeDtypeStruct((n, LANES), jnp.int32), jax.ShapeDtypeStruct((n, LANES), F32)),
        grid=(n // tm,),
        in_specs=[pl.BlockSpec((tm, wa), lambda i: (i, 0)),
                  pl.BlockSpec((tm, wa), lambda i: (i, A_OG // wa)),
                  pl.BlockSpec((tm, wb), lambda i: (i, 0)),
                  pl.BlockSpec((tm, wb), lambda i: (i, B_OG // wb)),
                  pl.BlockSpec((tm, wd), lambda i: (i, 0)),
                  const((1, LANES)), const((1, LANES)), const((D_MIX, d)),
                  pl.BlockSpec((tm, d), lambda i: (i, 0)),
                  const((MOD_ROWS, 6, d)), const((1, d)), const((d, LANES)), const((1, LANES))],
        out_specs=(pl.BlockSpec((tm, d), lambda i: (i, 0)), pl.BlockSpec((tm, d // 2), lambda i: (i, 0)),
                   pl.BlockSpec((tm, LANES), lambda i: (i, 0)), pl.BlockSpec((tm, LANES), lambda i: (i, 0))),
        compiler_params=_cparams(("parallel",)),
        name="out_proj",
    )(oa, p, ob, p, yd, gla_nw.reshape(1, LANES), gdn_nw.reshape(1, LANES), w_out, xt, mod,
      norm2_w.reshape(1, d), rw_pad, rb_pad)


MOE_SUB = 256
PERM = 256


MOE_UNROLL = 4


def _for_sub_blocks(code, blocks):
    n = code // 2
    full = lambda first, count: blocks(
        [pl.ds(pl.multiple_of((first + u) * MOE_SUB, MOE_SUB), MOE_SUB) for u in range(count)])
    groups = n // MOE_UNROLL

    def body(i, carry):
        full(i * MOE_UNROLL, MOE_UNROLL)
        return carry

    lax.fori_loop(0, groups, body, 0)
    done = groups * MOE_UNROLL
    step = MOE_UNROLL // 2
    while step:
        take = ((n - done) & step) != 0
        pl.when(take)(functools.partial(full, done, step))
        done = done + jnp.where(take, step, 0)
        step //= 2
    pl.when(code % 2 == 1)(
        lambda: blocks([pl.ds(pl.multiple_of(n * MOE_SUB, MOE_SUB // 2), MOE_SUB // 2)]))


def _unpack_rows(u):
    lo = lax.bitcast_convert_type(u << 16, F32).astype(BF16)
    hi = lax.bitcast_convert_type(u & jnp.uint32(0xFFFF0000), F32).astype(BF16)
    return jnp.concatenate([lo, hi], axis=1)


def _moe_kernel(ve_ref, vn_ref, nu_ref, x_ref, w1_ref, b1_ref, perm_ref, w2_ref, b2_ref, o_ref, wp_scr, w2b_scr):
    v = pl.program_id(0)
    tn = w1_ref.shape[3]
    d = o_ref.shape[1]
    live = vn_ref[v] > 0

    @pl.when(live)
    def _():
        for c in range(tn // PERM):
            wc = w1_ref[0, 0, :, c * PERM:(c + 1) * PERM].astype(BF16)
            wp_scr[:, c * PERM:(c + 1) * PERM] = _dot(wc, perm_ref[...]).astype(BF16)
        w2b_scr[...] = w2_ref[0, 0].astype(BF16)

    @pl.when(live & (pl.program_id(1) == 0))
    def _():
        o_ref[...] = jnp.broadcast_to(b2_ref[0], o_ref.shape)

    def blocks(rows):
        hids = [_dot(_unpack_rows(x_ref[r, :]), wp_scr[...]) for r in rows]
        acts = []
        for hid in hids:
            hid = hid + b1_ref[0]
            parts = []
            for c in range(tn // PERM):
                glu = jnp.minimum(hid[:, c * PERM:c * PERM + PERM // 2], SWIGLU_LIMIT)
                lin = jnp.clip(hid[:, c * PERM + PERM // 2:(c + 1) * PERM], -SWIGLU_LIMIT, SWIGLU_LIMIT)
                parts.append((glu * jax.nn.sigmoid(SWIGLU_ALPHA * glu) * (lin + 1.0)).astype(BF16))
            acts.append(jnp.concatenate(parts, axis=1))
        for nt in range(d // MXU_N if d >= MXU_N else 1):
            cols = slice(nt * MXU_N, min((nt + 1) * MXU_N, d))
            ys = [_dot(a, w2b_scr[:, cols]) for a in acts]
            for r, y in zip(rows, ys):
                o_ref[r, cols] += y

    _for_sub_blocks(vn_ref[v], blocks)


def _moe_experts(xs, vis_e, vis_n, n_used, w1, b1p, w2, b2, perm, *, layer, rows_per_visit):
    _, n_exp, d, dh2 = w1.shape
    r = rows_per_visit
    n_vis = xs.shape[0] // r
    tn = _pick(dh2, 512)
    n_j = dh2 // tn
    live = lambda v, nu: v < nu[0]
    jj = lambda v, j, nu: jnp.where(live(v, nu), j, n_j - 1)
    return pl.pallas_call(
        _moe_kernel,
        out_shape=jax.ShapeDtypeStruct(((n_vis + 1) * r, d), F32),
        grid_spec=pltpu.PrefetchScalarGridSpec(
            num_scalar_prefetch=3, grid=(n_vis, n_j),
            in_specs=[pl.BlockSpec((r, d // 2), lambda v, j, ve, vn, nu: (jnp.minimum(v, nu[0] - 1), 0)),
                      pl.BlockSpec((1, 1, d, tn), lambda v, j, ve, vn, nu: (layer, ve[v], 0, jj(v, j, nu))),
                      pl.BlockSpec((1, 1, tn), lambda v, j, ve, vn, nu: (ve[v], 0, jj(v, j, nu))),
                      pl.BlockSpec((PERM, PERM), lambda v, j, ve, vn, nu: (0, 0)),
                      pl.BlockSpec((1, 1, tn // 2, d), lambda v, j, ve, vn, nu: (layer, ve[v], jj(v, j, nu), 0)),
                      pl.BlockSpec((1, 1, d), lambda v, j, ve, vn, nu: (ve[v], 0, 0))],
            out_specs=pl.BlockSpec((r, d), lambda v, j, ve, vn, nu: (jnp.where(live(v, nu), v, n_vis), 0)),
            scratch_shapes=[pltpu.VMEM((d, tn), BF16), pltpu.VMEM((tn // 2, d), BF16)]),
        compiler_params=_cparams(("arbitrary", "arbitrary")),
        name="moe_experts",
    )(vis_e, vis_n, n_used, xs, w1, b1p, perm, w2, b2.reshape(n_exp, 1, d))


def _perm_matrix():
    src = jnp.arange(PERM)
    dst = jnp.where(src % 2 == 0, src // 2, PERM // 2 + src // 2)
    return (dst[:, None] == jnp.arange(PERM)[None, :]).astype(BF16)


def _deinterleave_bias(b1):
    n_exp, dh2 = b1.shape
    b = b1.reshape(n_exp, dh2 // PERM, PERM // 2, 2)
    return jnp.concatenate([b[..., 0], b[..., 1]], axis=-1).reshape(n_exp, 1, dh2)


def _routing(top_e, n_tok, rows_per_visit):
    r = rows_per_visit
    n_assign = n_tok * TOP_K
    n_vis = n_assign // r + N_EXPERTS
    flat_e = top_e.reshape(-1)
    onehot = (flat_e[:, None] == jnp.arange(N_EXPERTS)[None, :]).astype(jnp.int32)
    rank = jnp.take_along_axis(jnp.cumsum(onehot, axis=0) - onehot, flat_e[:, None], axis=1)[:, 0]
    counts = jnp.sum(onehot, axis=0)
    seg_vis = (counts + r - 1) // r
    vis_end = jnp.cumsum(seg_vis)
    vis_start = vis_end - seg_vis
    dest = vis_start[flat_e] * r + rank
    v = jnp.arange(n_vis)
    n_used = vis_end[-1]
    vc = jnp.minimum(v, n_used - 1)
    vis_e = jnp.minimum(jnp.searchsorted(vis_end, vc, side="right"), N_EXPERTS - 1).astype(jnp.int32)
    valid = jnp.clip(counts[vis_e] - (vc - vis_start[vis_e]) * r, 0, r)
    left = valid % MOE_SUB
    n_full = valid // MOE_SUB + (left > MOE_SUB // 2)
    half = (left > 0) & (left <= MOE_SUB // 2)
    vis_n = jnp.where(v < n_used, 2 * n_full + half, 0).astype(jnp.int32)
    tail = jnp.minimum(vis_start * r + counts // MOE_SUB * MOE_SUB, (n_vis * r) - MOE_SUB).astype(jnp.int32)
    return dest.astype(jnp.int32), tail, vis_e, vis_n, n_used.reshape(1).astype(jnp.int32), n_vis


def _dispatch_kernel(idx_ref, tail_ref, h_ref, xs_ref, zero_scr, zsem, sem):
    tm = h_ref.shape[0]

    def zero_copy(e):
        start = pl.multiple_of(tail_ref[0, e], MOE_SUB)
        return pltpu.make_async_copy(zero_scr, xs_ref.at[pl.ds(start, MOE_SUB), :], zsem)

    @pl.when(pl.program_id(0) == 0)
    def _():
        zero_scr[...] = jnp.zeros(zero_scr.shape, zero_scr.dtype)
        for e in range(N_EXPERTS):
            zero_copy(e).start()
        for e in range(N_EXPERTS):
            zero_copy(e).wait()

    def row_copy(t, k, row):
        return pltpu.make_async_copy(h_ref.at[pl.ds(t, 1), :], xs_ref.at[pl.ds(row, 1), :], sem)

    def issue(t, carry):
        for k in range(TOP_K):
            row_copy(t, k, idx_ref[0, 0, t * TOP_K + k]).start()
        return carry

    def drain(t, carry):
        for k in range(TOP_K):
            row_copy(t, k, 0).wait()
        return carry

    lax.fori_loop(0, tm, issue, 0)
    lax.fori_loop(0, tm, drain, 0)


def _dispatch(h2, dest, tail, *, n_rows, sub):
    n, d = h2.shape
    tm = min(sub, 128)
    return pl.pallas_call(
        _dispatch_kernel,
        out_shape=jax.ShapeDtypeStruct((n_rows, d), h2.dtype),
        grid=(n // tm,),
        in_specs=[pl.BlockSpec((1, 1, tm * TOP_K), lambda i: (i, 0, 0), memory_space=pltpu.SMEM),
                  pl.BlockSpec(memory_space=pltpu.SMEM),
                  pl.BlockSpec((tm, d), lambda i: (i, 0))],
        out_specs=pl.BlockSpec(memory_space=pl.ANY),
        scratch_shapes=[pltpu.VMEM((MOE_SUB, d), h2.dtype), pltpu.SemaphoreType.DMA(()), pltpu.SemaphoreType.DMA(())],
        compiler_params=_cparams(("arbitrary",)),
        name="moe_dispatch",
    )(dest.reshape(n // tm, 1, tm * TOP_K), tail.reshape(1, N_EXPERTS), h2)


def _combine_kernel(idx_ref, x_ref, g_ref, mod_ref, fw_ref, yb_ref, o_ref, buf, sem, *,
                    sub, subs_per_batch, n_batch, final):
    tm = x_ref.shape[0]

    def row_copy(t, k, row):
        return pltpu.make_async_copy(yb_ref.at[pl.ds(row, 1), :], buf.at[k, pl.ds(t, 1), :], sem)

    def issue(t, carry):
        for k in range(TOP_K):
            row_copy(t, k, idx_ref[0, 0, t * TOP_K + k]).start()
        return carry

    def drain(t, carry):
        for k in range(TOP_K):
            row_copy(t, k, 0).wait()
        return carry

    lax.fori_loop(0, tm, issue, 0)
    lax.fori_loop(0, tm, drain, 0)
    g = g_ref[...]
    y = g[:, 0:1] * buf[0]
    for k in range(1, TOP_K):
        y = y + g[:, k:k + 1] * buf[k]
    m = mod_ref[_group_of((pl.program_id(0) * tm) // sub, subs_per_batch, n_batch)]
    x_new = x_ref[...] + m[5:6] * y
    o_ref[...] = _rms(x_new, NORM_EPS) * fw_ref[...] if final else x_new


def _combine(xt, yb, dest, top_g, mod, final_w, *, sub, subs_per_batch, n_batch, final):
    n, d = xt.shape
    tm = min(sub, 128)
    kern = functools.partial(_combine_kernel, sub=sub, subs_per_batch=subs_per_batch, n_batch=n_batch, final=final)
    row = pl.BlockSpec((tm, d), lambda i: (i, 0))
    return pl.pallas_call(
        kern,
        out_shape=jax.ShapeDtypeStruct((n, d), F32),
        grid=(n // tm,),
        in_specs=[pl.BlockSpec((1, 1, tm * TOP_K), lambda i: (i, 0, 0), memory_space=pltpu.SMEM),
                  row, pl.BlockSpec((tm, LANES), lambda i: (i, 0)),
                  pl.BlockSpec((MOD_ROWS, 6, d), lambda i: (0, 0, 0)), pl.BlockSpec((1, d), lambda i: (0, 0)),
                  pl.BlockSpec(memory_space=pl.ANY)],
        out_specs=row,
        scratch_shapes=[pltpu.VMEM((TOP_K, tm, d), F32), pltpu.SemaphoreType.DMA(())],
        compiler_params=_cparams(("arbitrary",)),
        name="moe_combine",
    )(dest.reshape(n // tm, 1, tm * TOP_K), xt, top_g, mod, final_w.reshape(1, d), yb)


def _pad_cols(w, cuts):
    parts = []
    for start, width, padded in cuts:
        parts.append(w[:, start:start + width])
        if padded > width:
            parts.append(jnp.zeros((w.shape[0], padded - width), w.dtype))
    return jnp.concatenate(parts, axis=1)


def kernel(x, c, ctx, c_ctx, w_mod, b_mod, norm1_w, w_in, gla_gate_w, gla_gate_b, gla_norm_w, gdn_conv_w,
           gdn_a_log, gdn_dt_bias, gdn_norm_w, diff_lambda, diff_norm_w, w_out, norm2_w, router_w, router_b,
           expert_w1, expert_b1, expert_w2, expert_b2, final_norm_w):
    n_batch, s_len, d = x.shape
    l_ctx = ctx.shape[1]
    depth = w_mod.shape[0]
    t_len = l_ctx + s_len
    n_tok = n_batch * t_len
    subs_per_batch = t_len // l_ctx
    assert s_len % l_ctx == 0 and l_ctx % CHUNK == 0 and n_batch < MOD_ROWS and l_ctx % 16 == 0
    rows_per_visit = -(-(n_tok * TOP_K // N_EXPERTS * 9 // 8) // MOE_SUB) * MOE_SUB

    xt = jnp.concatenate([ctx, x], axis=1).reshape(n_tok, d)
    cvec = jnp.concatenate([c, c_ctx[None, :], jnp.zeros((MOD_ROWS - n_batch - 1, d), F32)], axis=0)
    mod_all = _modulation(cvec, w_mod, b_mod).reshape(depth, MOD_ROWS, 6, d)
    cos_t, sa_t, sb_t = _rope_tables(t_len, l_ctx)
    perm = _perm_matrix()
    a_w = GLA_HEADS * (2 * GLA_DK + 2 * GLA_DV)
    b_w = GDN_HEADS * (2 * GDN_DK + 2 * GDN_DV)
    c_w = DIFF_HEADS * (4 * DIFF_DH + DIFF_DV)
    glr0, b0 = a_w, a_w + 2 * GLA_RANK
    ba0 = b0 + b_w
    c0 = ba0 + 4 * GDN_HEADS
    cuts = ((0, a_w, a_w), (b0, b_w, b_w), (c0, c_w, c_w), (glr0, 2 * GLA_RANK, LANES), (ba0, 4 * GDN_HEADS, LANES))
    assert (a_w, a_w + b_w, a_w + b_w + c_w) == (B_Q, C_Q, A_GLR)

    for l in range(depth):
        lambda_init = 0.8 - 0.6 * math.exp(-0.3 * l)
        mod = mod_all[l]
        w_pad = _pad_cols(w_in[l], cuts).astype(BF16)
        p = _in_proj(xt, norm1_w[l], mod, w_pad, sub=l_ctx, subs_per_batch=subs_per_batch, n_batch=n_batch)

        gw = gla_gate_w[l]
        gw_pad = jnp.zeros((2, LANES, GLA_HEADS * GLA_DK), F32)
        gw_pad = gw_pad.at[0, 0:GLA_RANK].set(gw[0]).at[1, GLA_RANK:2 * GLA_RANK].set(gw[1])
        oa = _gla(p, gw_pad, gla_gate_b[l].reshape(2, 1, -1), n_batch=n_batch, t_len=t_len, l_ctx=l_ctx)

        ad = jnp.zeros((8, LANES), F32)
        ad = ad.at[0, 2 * GDN_HEADS:4 * GDN_HEADS].set(gdn_a_log[l].reshape(-1))
        ad = ad.at[1, 2 * GDN_HEADS:4 * GDN_HEADS].set(gdn_dt_bias[l].reshape(-1))
        ob = _gdn(p, gdn_conv_w[l], ad, n_batch=n_batch, t_len=t_len, l_ctx=l_ctx)

        qk = _rope(p, cos_t, sa_t, sb_t, tm=l_ctx)
        yd = _diff_attn(qk, p, diff_lambda[l], diff_norm_w[l], n_batch=n_batch, t_len=t_len, l_ctx=l_ctx,
                        lambda_init=lambda_init)

        rw_pad = jnp.zeros((d, LANES), F32).at[:, :N_EXPERTS].set(router_w[l])
        rb_pad = jnp.full((1, LANES), -1e30, F32).at[0, :N_EXPERTS].set(router_b[l])
        xt, h2, top_e, top_g = _out_proj(oa, ob, yd, p, gla_norm_w[l], gdn_norm_w[l], w_out[l].astype(BF16), xt, mod,
                                         norm2_w[l], rw_pad, rb_pad, tm=l_ctx, subs_per_batch=subs_per_batch,
                                         n_batch=n_batch)

        dest, tail, vis_e, vis_n, n_used, n_vis = _routing(top_e[:, :TOP_K], n_tok, rows_per_visit)
        xs = _dispatch(h2, dest, tail, n_rows=n_vis * rows_per_visit, sub=l_ctx)
        yb = _moe_experts(xs, vis_e, vis_n, n_used, expert_w1, _deinterleave_bias(expert_b1[l]),
                          expert_w2, expert_b2[l], perm, layer=l, rows_per_visit=rows_per_visit)
        xt = _combine(xt, yb, dest, top_g, mod, final_norm_w, sub=l_ctx, subs_per_batch=subs_per_batch,
                      n_batch=n_batch, final=(l == depth - 1))
    return xt.reshape(n_batch, t_len, d)[:, l_ctx:, :]
```

```python
import functools
import math

import jax
import jax.numpy as jnp
from jax import lax
from jax.experimental import pallas as pl
from jax.experimental.pallas import tpu as pltpu

F32 = jnp.float32
BF16 = jnp.bfloat16

GRID_W = 64
GLA_HEADS, GLA_DK, GLA_DV, GLA_RANK, GLA_GATE_NORM = 4, 64, 128, 16, 16.0
GDN_HEADS, GDN_DK, GDN_DV, GDN_CONV = 6, 128, 128, 5
DIFF_HEADS, DIFF_DH, DIFF_DV, DIFF_EPS = 6, 64, 128, 1e-5
ROPE_THETA = 10000.0
CHUNK = 64
N_EXPERTS, TOP_K = 32, 4
SWIGLU_ALPHA, SWIGLU_LIMIT = 1.702, 7.0
NORM_EPS = 1e-6
LANES = 128
MXU_N = 256
MOD_ROWS = 8

A_Q, A_K, A_V, A_OG = 0, 256, 512, 1024
B_Q, B_K, B_V, B_OG = 1536, 2304, 3072, 3840
C_Q, C_K, C_V = 4608, 5376, 6144
A_GLR, B_BA = 6912, 7040
NP_IN = 7168
D_MIX = GLA_HEADS * GLA_DV + GDN_HEADS * GDN_DV + DIFF_HEADS * DIFF_DV

VMEM_LIMIT = 56 * 1024 * 1024

def _dot(a, b):
    return jnp.dot(a, b, preferred_element_type=F32)


def _dot_nt(a, b):
    return lax.dot_general(a, b, (((1,), (1,)), ((), ())), preferred_element_type=F32)


def _split3(x):
    hi = x.astype(BF16)
    r = x - hi.astype(F32)
    mid = r.astype(BF16)
    return hi, mid, (r - mid.astype(F32)).astype(BF16)


def _dot_sel(sel, x, nt=False):
    dot = _dot_nt if nt else _dot
    s = sel.astype(F32).astype(BF16)
    hi, mid, lo = _split3(x)
    return dot(s, hi) + dot(s, mid) + dot(s, lo)


def _dot3(a, b):
    a_hi = a.astype(BF16)
    b_hi = b.astype(BF16)
    a_lo = (a - a_hi.astype(F32)).astype(BF16)
    b_lo = (b - b_hi.astype(F32)).astype(BF16)
    return _dot(a_hi, b_hi) + _dot(a_hi, b_lo) + _dot(a_lo, b_hi)


def _pick(n, pref):
    if n <= pref:
        return n
    t = pref - pref % LANES
    while t > LANES and n % t:
        t -= LANES
    return t


def _silu(x):
    return x * jax.nn.sigmoid(x)


def _log_sigmoid(x):
    return jnp.minimum(x, 0.0) - jnp.log(1.0 + jnp.exp(-jnp.abs(x)))


def _softplus(x):
    return jnp.maximum(x, 0.0) + jnp.log(1.0 + jnp.exp(-jnp.abs(x)))


def _cparams(sem):
    return pltpu.CompilerParams(dimension_semantics=sem, vmem_limit_bytes=VMEM_LIMIT)


def _mod_kernel(c_ref, w_ref, b_ref, o_ref):
    a = _silu(c_ref[...])
    o_ref[0] = _dot(a.astype(BF16), w_ref[0].astype(BF16)) + b_ref[0]


def _modulation(cvec, w_mod, b_mod):
    depth, d, n6 = w_mod.shape
    tn = _pick(n6, 1024)
    return pl.pallas_call(
        _mod_kernel,
        out_shape=jax.ShapeDtypeStruct((depth, MOD_ROWS, n6), F32),
        grid=(depth, n6 // tn),
        in_specs=[pl.BlockSpec((MOD_ROWS, d), lambda l, j: (0, 0)),
                  pl.BlockSpec((1, d, tn), lambda l, j: (l, 0, j)),
                  pl.BlockSpec((1, 1, tn), lambda l, j: (l, 0, j))],
        out_specs=pl.BlockSpec((1, MOD_ROWS, tn), lambda l, j: (l, 0, j)),
        compiler_params=_cparams(("parallel", "parallel")),
        name="modulation",
    )(cvec, w_mod, b_mod.reshape(depth, 1, n6))


def _group_of(sub_block, subs_per_batch, n_batch):
    return jnp.where(sub_block % subs_per_batch == 0, n_batch, sub_block // subs_per_batch)


def _rms(x, eps):
    return x * lax.rsqrt(jnp.mean(x * x, axis=-1, keepdims=True) + eps)


def _in_proj_kernel(x_ref, nw_ref, mod_ref, w_ref, o_ref, h_scr, *, sub, subs_per_batch, n_batch):
    i = pl.program_id(0)

    @pl.when(pl.program_id(1) == 0)
    def _():
        for s in range(x_ref.shape[0] // sub):
            m = mod_ref[_group_of(i * (x_ref.shape[0] // sub) + s, subs_per_batch, n_batch)]
            y = _rms(x_ref[s * sub:(s + 1) * sub, :], NORM_EPS) * nw_ref[...]
            h_scr[s * sub:(s + 1) * sub, :] = (y * (1.0 + m[1:2]) + m[0:1]).astype(BF16)

    o_ref[...] = _dot(h_scr[...], w_ref[...])


def _in_proj(xt, norm_w, mod, w_pad, *, sub, subs_per_batch, n_batch):
    n, d = xt.shape
    npad = w_pad.shape[1]
    tm = sub * 2 if (n // sub) % 2 == 0 else sub
    tn = _pick(npad, 1024)
    kern = functools.partial(_in_proj_kernel, sub=sub, subs_per_batch=subs_per_batch, n_batch=n_batch)
    return pl.pallas_call(
        kern,
        out_shape=jax.ShapeDtypeStruct((n, npad), F32),
        grid=(n // tm, npad // tn),
        in_specs=[pl.BlockSpec((tm, d), lambda i, j: (i, 0)),
                  pl.BlockSpec((1, d), lambda i, j: (0, 0)),
                  pl.BlockSpec((MOD_ROWS, 6, d), lambda i, j: (0, 0, 0)),
                  pl.BlockSpec((d, tn), lambda i, j: (0, j))],
        out_specs=pl.BlockSpec((tm, tn), lambda i, j: (i, j)),
        scratch_shapes=[pltpu.VMEM((tm, d), BF16)],
        compiler_params=_cparams(("parallel", "arbitrary")),
        name="in_proj",
    )(xt, norm_w.reshape(1, d), mod, w_pad)


def _tri_masks(c):
    r = lax.broadcasted_iota(jnp.int32, (c, c), 0)
    s = lax.broadcasted_iota(jnp.int32, (c, c), 1)
    return ((s <= r, s < r), (s >= r, s > r))


def _bwd_chunk(i, nc_ctx, nc):
    return jnp.where(i < nc_ctx, nc_ctx - 1 - i, nc - 1 - (i - nc_ctx))


def _gla_kernel(q_ref, k_ref, v_ref, glr_ref, gw_ref, gb_ref, o_ref, st_scr, *, nc_ctx, group):
    t_len = q_ref.shape[0]
    nc = t_len // CHUNK
    masks = _tri_masks(CHUNK)
    lane = lax.broadcasted_iota(jnp.int32, (1, LANES), 1)
    head_mask = [(lane < GLA_DK).astype(F32), (lane >= GLA_DK).astype(F32)]
    o_ref[...] = jnp.zeros(o_ref.shape, F32)
    st_scr[...] = jnp.zeros(st_scr.shape, F32)

    def step(gi, carry):
        zu = [(z, u) for z in range(2) for u in range(group)]
        cs = [gi * group + u if z == 0 else _bwd_chunk(gi * group + u, nc_ctx, nc) for z, u in zu]
        rows = [pl.ds(pl.multiple_of(c * CHUNK, CHUNK), CHUNK) for c in cs]
        pre = [_dot3(glr_ref[rows[n], :], gw_ref[z]) + gb_ref[z] for n, (z, u) in enumerate(zu)]
        b = [_dot_sel(masks[z][0], _log_sigmoid(pre[n]) / GLA_GATE_NORM) for n, (z, u) in enumerate(zu)]
        tot = [b[n][CHUNK - 1:CHUNK, :] if z == 0 else b[n][0:1, :] for n, (z, u) in enumerate(zu)]
        qd, kk, kd = [], [], []
        for n in range(len(zu)):
            k = k_ref[rows[n], :]
            qd.append(q_ref[rows[n], :] * GLA_DK ** -0.5 * jnp.exp(b[n]))
            kk.append((k * jnp.exp(-b[n])).astype(BF16))
            kd.append(k * jnp.exp(tot[n] - b[n]))
        nj = [(n, j) for n in range(len(zu)) for j in range(2)]
        qdj = [(qd[n] * head_mask[j]).astype(BF16) for n, j in nj]
        vj = [v_ref[rows[n], j * GLA_DV:(j + 1) * GLA_DV] for n, j in nj]
        att = [jnp.where(masks[zu[n][0]][0], _dot_nt(qdj[m], kk[n]), 0.0).astype(BF16) for m, (n, j) in enumerate(nj)]
        intra = [_dot(att[m], vj[m].astype(BF16)) for m in range(len(nj))]
        upd = [_dot(vj[m].T.astype(BF16), (kd[n] * head_mask[j]).astype(BF16)) for m, (n, j) in enumerate(nj)]
        for z in range(2):
            for j in range(2):
                st = st_scr[z, j]
                for u in range(group):
                    n = z * group + u
                    m = 2 * n + j
                    o_ref[rows[n], j * GLA_DV:(j + 1) * GLA_DV] += intra[m] + _dot_nt(qdj[m], st.astype(BF16))
                    st = st * jnp.exp(tot[n]) + upd[m]
                st_scr[z, j] = st
        return carry

    lax.fori_loop(0, nc // group, step, 0)


def _gla(p, gw_pad, gb, *, n_batch, t_len, l_ctx):
    n = p.shape[0]
    spec = lambda col, w: pl.BlockSpec((t_len, w), lambda b, hp, col=col, w=w: (b, col // w + hp))
    group = next(g for g in (4, 2, 1) if (t_len // CHUNK) % g == 0)
    kern = functools.partial(_gla_kernel, nc_ctx=l_ctx // CHUNK, group=group)
    return pl.pallas_call(
        kern,
        out_shape=jax.ShapeDtypeStruct((n, GLA_HEADS * GLA_DV), F32),
        grid=(n_batch, GLA_HEADS // 2),
        in_specs=[spec(A_Q, LANES), spec(A_K, LANES), spec(A_V, 2 * GLA_DV),
                  pl.BlockSpec((t_len, LANES), lambda b, hp: (b, A_GLR // LANES)),
                  pl.BlockSpec((2, LANES, LANES), lambda b, hp: (0, 0, hp)),
                  pl.BlockSpec((2, 1, LANES), lambda b, hp: (0, 0, hp))],
        out_specs=pl.BlockSpec((t_len, 2 * GLA_DV), lambda b, hp: (b, hp)),
        scratch_shapes=[pltpu.VMEM((2, 2, GLA_DV, LANES), F32)],
        compiler_params=_cparams(("parallel", "parallel")),
        name="gla_scan",
    )(p, p, p, p, gw_pad, gb)


def _gdn_kernel(q_ref, k_ref, v_ref, ba_ref, cwq_ref, cwk_ref, cwv_ref, ad_ref, o_ref,
                m_scr, b_scr, qp_scr, dec_scr, s_scr, *, nc_ctx, l_ctx, group):
    t_len = q_ref.shape[0]
    nc = t_len // CHUNK
    h = pl.program_id(1)
    masks = _tri_masks(CHUNK)
    lane = lax.broadcasted_iota(jnp.int32, (1, LANES), 1)
    halo = 8
    pad = (GDN_CONV - 1) // 2
    ones_l0 = jnp.broadcast_to((lane == 0).astype(F32), (CHUNK, LANES))

    def tap_masks(r0):
        row = r0 + lax.broadcasted_iota(jnp.int32, (CHUNK, LANES), 0)
        return [(((row + d) >= l_ctx) == (row >= l_ctx)) & (row + d >= 0) & (row + d < t_len)
                for d in range(-pad, pad + 1)]

    def conv(src_ref, cw_ref, r0, ok):
        top = src_ref[pl.ds(pl.multiple_of(jnp.maximum(r0 - halo, 0), halo), halo), :]
        bot = src_ref[pl.ds(pl.multiple_of(jnp.minimum(r0 + CHUNK, t_len - halo), halo), halo), :]
        win = jnp.concatenate([top, src_ref[pl.ds(r0, CHUNK), :], bot], axis=0)
        acc = jnp.zeros((CHUNK, LANES), F32)
        for j in range(GDN_CONV):
            tap = win[halo + j - pad:halo + j - pad + CHUNK, :]
            acc = acc + jnp.where(ok[j], tap, 0.0) * cw_ref[j:j + 1, :]
        return _silu(acc)

    def l2n(y):
        return y * lax.rsqrt(jnp.sum(y * y, axis=-1, keepdims=True) + 1e-6)

    def prep(gi, carry):
        per_chunk, chains = [], []
        for cc in range(group):
            c = gi * group + cc
            r0 = pl.multiple_of(c * CHUNK, CHUNK)
            ok = tap_masks(r0)
            q = l2n(conv(q_ref, cwq_ref, r0, ok)) * GDN_DK ** -0.5
            k = l2n(conv(k_ref, cwk_ref, r0, ok))
            v = conv(v_ref, cwv_ref, r0, ok)
            ba = ba_ref[pl.ds(r0, CHUNK), :]
            beta_all = jax.nn.sigmoid(ba)
            g_all = -jnp.exp(ad_ref[0:1, :]) * _softplus(ba + ad_ref[1:2, :])
            per_chunk.append((c, r0, q, k, v, beta_all, g_all))
        grams = []
        for (_, _, q, k, _, _, _) in per_chunk:
            kbf = k.astype(BF16)
            grams.append((_dot_nt(kbf, kbf), _dot_nt(q.astype(BF16), kbf)))
        gcs = []
        for (_, _, _, _, _, beta_all, g_all) in per_chunk:
            for z in range(2):
                col = z * GDN_HEADS + h
                beta = jnp.sum(jnp.where(lane == col, beta_all, 0.0), axis=-1, keepdims=True)
                g = jnp.sum(jnp.where(lane == 2 * GDN_HEADS + col, g_all, 0.0), axis=-1, keepdims=True)
                gcs.append((beta, _dot_sel(masks[z][0], jnp.broadcast_to(g, (CHUNK, LANES)))))
        grows = [_dot_sel(ones_l0, gc, nt=True) for (_, gc) in gcs]
        xs, npows, aqk, qg, kdt, where = [], [], [], [], [], []
        for ci, (c, r0, q, k, v, _, _) in enumerate(per_chunk):
            kk, qk = grams[ci]
            for z in range(2):
                incl, strict = masks[z]
                beta, gc = gcs[2 * ci + z]
                dmat = jnp.exp(jnp.where(incl, gc[:, :CHUNK] - grows[2 * ci + z], -jnp.inf))
                tot = gc[CHUNK - 1:CHUNK, :] if z == 0 else gc[0:1, :]
                where.append((z, c, r0))
                aqk.append(jnp.where(incl, qk * dmat, 0.0).astype(BF16))
                qg.append(q * jnp.exp(gc))
                kdt.append((k * jnp.exp(tot - gc)).T.astype(BF16))
                dec_scr[z, pl.ds(pl.multiple_of(c * halo, halo), halo), :] = jnp.broadcast_to(jnp.exp(tot), (halo, LANES))
                npows.append(-jnp.where(strict, beta * kk * dmat, 0.0))
                xs.append(jnp.concatenate([v * beta, k * (beta * jnp.exp(gc))], axis=1))
        for lvl in range(6):
            xs = [x + _dot3(npw, x) for x, npw in zip(xs, npows)]
            if lvl < 5:
                npows = [_dot3(npw, npw) for npw in npows]
        ub = [x[:, :GDN_DV].astype(BF16) for x in xs]
        wb = [x[:, GDN_DV:].astype(BF16) for x in xs]
        m_all = [_dot(kdt[n], wb[n]) for n in range(len(xs))]
        b_all = [_dot(kdt[n], ub[n]) for n in range(len(xs))]
        aw = [_dot(aqk[n], wb[n]) for n in range(len(xs))]
        au = [_dot(aqk[n], ub[n]) for n in range(len(xs))]
        for n, (z, c, r0) in enumerate(where):
            blk = pl.ds(pl.multiple_of(c * GDN_DK, GDN_DK), GDN_DK)
            m_scr[z, blk, :] = (-m_all[n]).astype(BF16)
            b_scr[z, blk, :] = b_all[n]
            qp_scr[z, pl.ds(r0, CHUNK), :] = (qg[n] - aw[n]).astype(BF16)
            o_ref[pl.ds(r0, CHUNK), :] += au[n]
        return carry

    o_ref[...] = jnp.zeros(o_ref.shape, F32)
    lax.fori_loop(0, nc // group, prep, 0)
    s_scr[...] = jnp.zeros(s_scr.shape, F32)

    def step(i, carry):
        cs = (i, _bwd_chunk(i, nc_ctx, nc))
        s = [s_scr[z] for z in range(2)]
        sb = [s[z].astype(BF16) for z in range(2)]
        blk = [pl.ds(pl.multiple_of(cs[z] * GDN_DK, GDN_DK), GDN_DK) for z in range(2)]
        rows = [pl.ds(pl.multiple_of(cs[z] * CHUNK, CHUNK), CHUNK) for z in range(2)]
        ms = [_dot(m_scr[z, blk[z], :], sb[z]) for z in range(2)]
        qs = [_dot(qp_scr[z, rows[z], :], sb[z]) for z in range(2)]
        for z in range(2):
            dec = dec_scr[z, pl.ds(pl.multiple_of(cs[z] * halo, halo), 1), :]
            s_scr[z] = s[z] * dec + ms[z] + b_scr[z, blk[z], :]
            o_ref[rows[z], :] += qs[z]
        return carry

    lax.fori_loop(0, nc, step, 0)


def _gdn(p, conv_w, ad, *, n_batch, t_len, l_ctx):
    n = p.shape[0]
    spec = lambda col: pl.BlockSpec((t_len, LANES), lambda b, h, col=col: (b, col // LANES + h))
    cspec = lambda off: pl.BlockSpec((GDN_CONV, LANES), lambda b, h, off=off: (0, off + h))
    nc = t_len // CHUNK
    group = next(g for g in (4, 2, 1) if nc % g == 0)
    kern = functools.partial(_gdn_kernel, nc_ctx=l_ctx // CHUNK, l_ctx=l_ctx, group=group)
    return pl.pallas_call(
        kern,
        out_shape=jax.ShapeDtypeStruct((n, GDN_HEADS * GDN_DV), F32),
        grid=(n_batch, GDN_HEADS),
        in_specs=[spec(B_Q), spec(B_K), spec(B_V),
                  pl.BlockSpec((t_len, LANES), lambda b, h: (b, B_BA // LANES)),
                  cspec(0), cspec(GDN_HEADS), cspec(2 * GDN_HEADS),
                  pl.BlockSpec((8, LANES), lambda b, h: (0, 0))],
        out_specs=pl.BlockSpec((t_len, GDN_DV), lambda b, h: (b, h)),
        scratch_shapes=[pltpu.VMEM((2, nc * GDN_DK, GDN_DK), BF16),
                        pltpu.VMEM((2, nc * GDN_DK, GDN_DV), F32),
                        pltpu.VMEM((2, t_len, GDN_DK), BF16),
                        pltpu.VMEM((2, nc * 8, LANES), F32),
                        pltpu.VMEM((2, GDN_DK, GDN_DV), F32)],
        compiler_params=_cparams(("parallel", "parallel")),
        name="gdn_scan",
    )(p, p, p, p, conv_w, conv_w, conv_w, ad)


def _rope_kernel(x_ref, cos_ref, sa_ref, sb_ref, o_ref):
    scale = jnp.where(pl.program_id(1) == 0, DIFF_DH ** -0.5, 1.0)
    cos, sa, sb = cos_ref[...] * scale, sa_ref[...] * scale, sb_ref[...] * scale
    for hh in range(DIFF_HEADS):
        x = x_ref[:, hh * LANES:(hh + 1) * LANES]
        y = x * cos + pltpu.roll(x, LANES - 16, 1) * sa + pltpu.roll(x, 16, 1) * sb
        o_ref[:, hh * LANES:(hh + 1) * LANES] = y.astype(BF16)


def _rope(p, cos_t, sa_t, sb_t, *, tm):
    n = p.shape[0]
    tblocks = cos_t.shape[0] // tm
    wq = DIFF_HEADS * LANES
    tspec = pl.BlockSpec((tm, LANES), lambda i, j: (i % tblocks, 0))
    return pl.pallas_call(
        _rope_kernel,
        out_shape=jax.ShapeDtypeStruct((n, 2 * wq), BF16),
        grid=(n // tm, 2),
        in_specs=[pl.BlockSpec((tm, wq), lambda i, j: (i, C_Q // wq + j)), tspec, tspec, tspec],
        out_specs=pl.BlockSpec((tm, wq), lambda i, j: (i, j)),
        compiler_params=_cparams(("parallel", "parallel")),
        name="rope",
    )(p, cos_t, sa_t, sb_t)


def _rope_tables(t_len, l_ctx):
    pos = jnp.arange(t_len - l_ctx)
    row = (pos // GRID_W).astype(F32)
    colp = (pos % GRID_W).astype(F32)
    half = DIFF_DH // 2
    inv = 1.0 / (ROPE_THETA ** (jnp.arange(0, half, 2, dtype=F32) / half))
    lane = jnp.arange(LANES)
    part = (lane % DIFF_DH) // half
    upper = (lane % half) // (half // 2)
    ang = jnp.where(part[None, :] == 0, row[:, None], colp[:, None]) * inv[lane % (half // 2)][None, :]
    cos, sin = jnp.cos(ang), jnp.sin(ang)
    sa = jnp.where(upper[None, :] == 0, -sin, 0.0)
    sb = jnp.where(upper[None, :] == 1, sin, 0.0)
    ident = lambda v, fill: jnp.concatenate([jnp.full((l_ctx, LANES), fill, F32), v.astype(F32)], axis=0)
    return ident(cos, 1.0), ident(sa, 0.0), ident(sb, 0.0)


def _attn_kernel(q_ref, k_ref, v_ref, lam_ref, nw_ref, o_ref, *, l_ctx, lambda_init):
    tq = q_ref.shape[0]
    t_len = k_ref.shape[0]
    lv = lam_ref[...]
    lam = (jnp.exp(jnp.sum(lv[0:1] * lv[1:2], axis=-1, keepdims=True))
           - jnp.exp(jnp.sum(lv[2:3] * lv[3:4], axis=-1, keepdims=True)) + lambda_init)
    lane = lax.broadcasted_iota(jnp.int32, (1, LANES), 1)

    def attend(kv_len):
        q = q_ref[...]
        k = k_ref[0:kv_len, :]
        v = v_ref[0:kv_len, :].astype(BF16)
        es, inv = [], []
        for m in range(2):
            qm = jnp.where((lane >= DIFF_DH) == (m == 1), q, jnp.zeros_like(q))
            s = _dot_nt(qm, k)
            e = jnp.exp(s - jnp.max(s, axis=-1, keepdims=True))
            es.append(e)
            inv.append(1.0 / jnp.sum(e, axis=-1, keepdims=True))
        w = es[0] * inv[0] - es[1] * (lam * inv[1])
        o = _dot(w.astype(BF16), v)
        o_ref[...] = _rms(o, DIFF_EPS) * nw_ref[...] * (1.0 - lambda_init)

    is_ctx = pl.program_id(2) < l_ctx // tq
    pl.when(is_ctx)(lambda: attend(l_ctx))
    pl.when(jnp.logical_not(is_ctx))(lambda: attend(t_len))


def _diff_attn(qk, p, lam_vecs, norm_w, *, n_batch, t_len, l_ctx, lambda_init):
    n = p.shape[0]
    tq = min(l_ctx, 256)
    nt = t_len // tq
    kern = functools.partial(_attn_kernel, l_ctx=l_ctx, lambda_init=lambda_init)
    return pl.pallas_call(
        kern,
        out_shape=jax.ShapeDtypeStruct((n, DIFF_HEADS * DIFF_DV), F32),
        grid=(n_batch, DIFF_HEADS, nt),
        in_specs=[pl.BlockSpec((tq, LANES), lambda b, h, t: (b * nt + t, h)),
                  pl.BlockSpec((t_len, LANES), lambda b, h, t: (b, DIFF_HEADS + h)),
                  pl.BlockSpec((t_len, LANES), lambda b, h, t: (b, C_V // LANES + h)),
                  pl.BlockSpec((4, DIFF_DH), lambda b, h, t: (0, 0)),
                  pl.BlockSpec((1, DIFF_DV), lambda b, h, t: (0, 0))],
        out_specs=pl.BlockSpec((tq, DIFF_DV), lambda b, h, t: (b * nt + t, h)),
        compiler_params=_cparams(("parallel", "parallel", "parallel")),
        name="diff_attn",
    )(qk, qk, p, lam_vecs, norm_w.reshape(1, DIFF_DV))


def _out_proj_kernel(oa_ref, oga_ref, ob_ref, ogb_ref, yd_ref, anw_ref, bnw_ref, w_ref, x_ref, mod_ref,
                     n2_ref, rw_ref, rb_ref, xo_ref, h2_ref, te_ref, tg_ref, *, subs_per_batch, n_batch):
    m = mod_ref[_group_of(pl.program_id(0), subs_per_batch, n_batch)]
    pieces = []
    for o_ref, og_ref, nw_ref, heads in ((oa_ref, oga_ref, anw_ref, GLA_HEADS), (ob_ref, ogb_ref, bnw_ref, GDN_HEADS)):
        for hh in range(heads):
            sl = slice(hh * LANES, (hh + 1) * LANES)
            pieces.append((_rms(o_ref[:, sl], NORM_EPS) * nw_ref[...] * _silu(og_ref[:, sl])).astype(BF16))
    pieces.append(yd_ref[...].astype(BF16))
    mix = jnp.concatenate(pieces, axis=1)
    x_new = x_ref[...] + m[2:3] * _dot(mix, w_ref[...])
    xo_ref[...] = x_new
    h2 = _rms(x_new, NORM_EPS) * n2_ref[...] * (1.0 + m[4:5]) + m[3:4]
    bits = lax.bitcast_convert_type(h2.astype(BF16).astype(F32), jnp.uint32)
    half = bits.shape[1] // 2
    h2_ref[...] = (bits[:, :half] >> 16) | (bits[:, half:] & jnp.uint32(0xFFFF0000))
    lg = _dot3(h2, rw_ref[...]) + rb_ref[...]
    lane = lax.broadcasted_iota(jnp.int32, lg.shape, 1)
    lane_f = lane.astype(F32)
    vals, idxs = [], []
    for _ in range(TOP_K):
        mx = jnp.max(lg, axis=-1, keepdims=True)
        ix = jnp.min(jnp.where(lg == mx, lane_f, float(LANES)), axis=-1, keepdims=True)
        vals.append(mx)
        idxs.append(ix)
        lg = jnp.where(lane_f == ix, -jnp.inf, lg)
    ex = [jnp.exp(v - vals[0]) for v in vals]
    inv_den = 1.0 / (ex[0] + ex[1] + ex[2] + ex[3])
    te = jnp.zeros(lg.shape, F32)
    tg = jnp.zeros(lg.shape, F32)
    for kk in range(TOP_K):
        te = jnp.where(lane == kk, idxs[kk], te)
        tg = jnp.where(lane == kk, ex[kk] * inv_den, tg)
    te_ref[...] = te.astype(jnp.int32)
    tg_ref[...] = tg


def _out_proj(oa, ob, yd, p, gla_nw, gdn_nw, w_out, xt, mod, norm2_w, rw_pad, rb_pad, *, tm, subs_per_batch, n_batch):
    n, d = xt.shape
    kern = functools.partial(_out_proj_kernel, subs_per_batch=subs_per_batch, n_batch=n_batch)
    const = lambda shape: pl.BlockSpec(shape, lambda i: (0,) * len(shape))
    wa, wb, wd = GLA_HEADS * GLA_DV, GDN_HEADS * GDN_DV, DIFF_HEADS * DIFF_DV
    return pl.pallas_call(
        kern,
        out_shape=(jax.ShapeDtypeStruct((n, d), F32), jax.ShapeDtypeStruct((n, d // 2), jnp.uint32),
                   jax.ShapeDtypeStruct((n, LANES), jnp.int32), jax.ShapeDtypeStruct((n, LANES), F32)),
        grid=(n // tm,),
        in_specs=[pl.BlockSpec((tm, wa), lambda i: (i, 0)),
                  pl.BlockSpec((tm, wa), lambda i: (i, A_OG // wa)),
                  pl.BlockSpec((tm, wb), lambda i: (i, 0)),
                  pl.BlockSpec((tm, wb), lambda i: (i, B_OG // wb)),
                  pl.BlockSpec((tm, wd), lambda i: (i, 0)),
                  const((1, LANES)), const((1, LANES)), const((D_MIX, d)),
                  pl.BlockSpec((tm, d), lambda i: (i, 0)),
                  const((MOD_ROWS, 6, d)), const((1, d)), const((d, LANES)), const((1, LANES))],
        out_specs=(pl.BlockSpec((tm, d), lambda i: (i, 0)), pl.BlockSpec((tm, d // 2), lambda i: (i, 0)),
                   pl.BlockSpec((tm, LANES), lambda i: (i, 0)), pl.BlockSpec((tm, LANES), lambda i: (i, 0))),
        compiler_params=_cparams(("parallel",)),
        name="out_proj",
    )(oa, p, ob, p, yd, gla_nw.reshape(1, LANES), gdn_nw.reshape(1, LANES), w_out, xt, mod,
      norm2_w.reshape(1, d), rw_pad, rb_pad)


MOE_SUB = 256
PERM = 256


MOE_UNROLL = 4


def _for_sub_blocks(code, blocks):
    n = code // 2
    full = lambda first, count: blocks(
        [pl.ds(pl.multiple_of((first + u) * MOE_SUB, MOE_SUB), MOE_SUB) for u in range(count)])
    groups = n // MOE_UNROLL

    def body(i, carry):
        full(i * MOE_UNROLL, MOE_UNROLL)
        return carry

    lax.fori_loop(0, groups, body, 0)
    done = groups * MOE_UNROLL
    step = MOE_UNROLL // 2
    while step:
        take = ((n - done) & step) != 0
        pl.when(take)(functools.partial(full, done, step))
        done = done + jnp.where(take, step, 0)
        step //= 2
    pl.when(code % 2 == 1)(
        lambda: blocks([pl.ds(pl.multiple_of(n * MOE_SUB, MOE_SUB // 2), MOE_SUB // 2)]))


def _unpack_rows(u):
    lo = lax.bitcast_convert_type(u << 16, F32).astype(BF16)
    hi = lax.bitcast_convert_type(u & jnp.uint32(0xFFFF0000), F32).astype(BF16)
    return jnp.concatenate([lo, hi], axis=1)


def _moe_kernel(ve_ref, vn_ref, nu_ref, x_ref, w1_ref, b1_ref, perm_ref, w2_ref, b2_ref, o_ref, wp_scr, w2b_scr):
    v = pl.program_id(0)
    tn = w1_ref.shape[3]
    d = o_ref.shape[1]
    live = vn_ref[v] > 0

    @pl.when(live)
    def _():
        for c in range(tn // PERM):
            wc = w1_ref[0, 0, :, c * PERM:(c + 1) * PERM].astype(BF16)
            wp_scr[:, c * PERM:(c + 1) * PERM] = _dot(wc, perm_ref[...]).astype(BF16)
        w2b_scr[...] = w2_ref[0, 0].astype(BF16)

    @pl.when(live & (pl.program_id(1) == 0))
    def _():
        o_ref[...] = jnp.broadcast_to(b2_ref[0], o_ref.shape)

    def blocks(rows):
        hids = [_dot(_unpack_rows(x_ref[r, :]), wp_scr[...]) for r in rows]
        acts = []
        for hid in hids:
            hid = hid + b1_ref[0]
            parts = []
            for c in range(tn // PERM):
                glu = jnp.minimum(hid[:, c * PERM:c * PERM + PERM // 2], SWIGLU_LIMIT)
                lin = jnp.clip(hid[:, c * PERM + PERM // 2:(c + 1) * PERM], -SWIGLU_LIMIT, SWIGLU_LIMIT)
                parts.append((glu * jax.nn.sigmoid(SWIGLU_ALPHA * glu) * (lin + 1.0)).astype(BF16))
            acts.append(jnp.concatenate(parts, axis=1))
        for nt in range(d // MXU_N if d >= MXU_N else 1):
            cols = slice(nt * MXU_N, min((nt + 1) * MXU_N, d))
            ys = [_dot(a, w2b_scr[:, cols]) for a in acts]
            for r, y in zip(rows, ys):
                o_ref[r, cols] += y

    _for_sub_blocks(vn_ref[v], blocks)


def _moe_experts(xs, vis_e, vis_n, n_used, w1, b1p, w2, b2, perm, *, layer, rows_per_visit):
    _, n_exp, d, dh2 = w1.shape
    r = rows_per_visit
    n_vis = xs.shape[0] // r
    tn = _pick(dh2, 512)
    n_j = dh2 // tn
    live = lambda v, nu: v < nu[0]
    jj = lambda v, j, nu: jnp.where(live(v, nu), j, n_j - 1)
    return pl.pallas_call(
        _moe_kernel,
        out_shape=jax.ShapeDtypeStruct(((n_vis + 1) * r, d), F32),
        grid_spec=pltpu.PrefetchScalarGridSpec(
            num_scalar_prefetch=3, grid=(n_vis, n_j),
            in_specs=[pl.BlockSpec((r, d // 2), lambda v, j, ve, vn, nu: (jnp.minimum(v, nu[0] - 1), 0)),
                      pl.BlockSpec((1, 1, d, tn), lambda v, j, ve, vn, nu: (layer, ve[v], 0, jj(v, j, nu))),
                      pl.BlockSpec((1, 1, tn), lambda v, j, ve, vn, nu: (ve[v], 0, jj(v, j, nu))),
                      pl.BlockSpec((PERM, PERM), lambda v, j, ve, vn, nu: (0, 0)),
                      pl.BlockSpec((1, 1, tn // 2, d), lambda v, j, ve, vn, nu: (layer, ve[v], jj(v, j, nu), 0)),
                      pl.BlockSpec((1, 1, d), lambda v, j, ve, vn, nu: (ve[v], 0, 0))],
            out_specs=pl.BlockSpec((r, d), lambda v, j, ve, vn, nu: (jnp.where(live(v, nu), v, n_vis), 0)),
            scratch_shapes=[pltpu.VMEM((d, tn), BF16), pltpu.VMEM((tn // 2, d), BF16)]),
        compiler_params=_cparams(("arbitrary", "arbitrary")),
        name="moe_experts",
    )(vis_e, vis_n, n_used, xs, w1, b1p, perm, w2, b2.reshape(n_exp, 1, d))


def _perm_matrix():
    src = jnp.arange(PERM)
    dst = jnp.where(src % 2 == 0, src // 2, PERM // 2 + src // 2)
    return (dst[:, None] == jnp.arange(PERM)[None, :]).astype(BF16)


def _deinterleave_bias(b1):
    n_exp, dh2 = b1.shape
    b = b1.reshape(n_exp, dh2 // PERM, PERM // 2, 2)
    return jnp.concatenate([b[..., 0], b[..., 1]], axis=-1).reshape(n_exp, 1, dh2)


def _routing(top_e, n_tok, rows_per_visit):
    r = rows_per_visit
    n_assign = n_tok * TOP_K
    n_vis = n_assign // r + N_EXPERTS
    flat_e = top_e.reshape(-1)
    onehot = (flat_e[:, None] == jnp.arange(N_EXPERTS)[None, :]).astype(jnp.int32)
    rank = jnp.take_along_axis(jnp.cumsum(onehot, axis=0) - onehot, flat_e[:, None], axis=1)[:, 0]
    counts = jnp.sum(onehot, axis=0)
    seg_vis = (counts + r - 1) // r
    vis_end = jnp.cumsum(seg_vis)
    vis_start = vis_end - seg_vis
    dest = vis_start[flat_e] * r + rank
    v = jnp.arange(n_vis)
    n_used = vis_end[-1]
    vc = jnp.minimum(v, n_used - 1)
    vis_e = jnp.minimum(jnp.searchsorted(vis_end, vc, side="right"), N_EXPERTS - 1).astype(jnp.int32)
    valid = jnp.clip(counts[vis_e] - (vc - vis_start[vis_e]) * r, 0, r)
    left = valid % MOE_SUB
    n_full = valid // MOE_SUB + (left > MOE_SUB // 2)
    half = (left > 0) & (left <= MOE_SUB // 2)
    vis_n = jnp.where(v < n_used, 2 * n_full + half, 0).astype(jnp.int32)
    tail = jnp.minimum(vis_start * r + counts // MOE_SUB * MOE_SUB, (n_vis * r) - MOE_SUB).astype(jnp.int32)
    return dest.astype(jnp.int32), tail, vis_e, vis_n, n_used.reshape(1).astype(jnp.int32), n_vis


def _dispatch_kernel(idx_ref, tail_ref, h_ref, xs_ref, zero_scr, zsem, sem):
    tm = h_ref.shape[0]

    def zero_copy(e):
        start = pl.multiple_of(tail_ref[0, e], MOE_SUB)
        return pltpu.make_async_copy(zero_scr, xs_ref.at[pl.ds(start, MOE_SUB), :], zsem)

    @pl.when(pl.program_id(0) == 0)
    def _():
        zero_scr[...] = jnp.zeros(zero_scr.shape, zero_scr.dtype)
        for e in range(N_EXPERTS):
            zero_copy(e).start()
        for e in range(N_EXPERTS):
            zero_copy(e).wait()

    def row_copy(t, k, row):
        return pltpu.make_async_copy(h_ref.at[pl.ds(t, 1), :], xs_ref.at[pl.ds(row, 1), :], sem)

    def issue(t, carry):
        for k in range(TOP_K):
            row_copy(t, k, idx_ref[0, 0, t * TOP_K + k]).start()
        return carry

    def drain(t, carry):
        for k in range(TOP_K):
            row_copy(t, k, 0).wait()
        return carry

    lax.fori_loop(0, tm, issue, 0)
    lax.fori_loop(0, tm, drain, 0)


def _dispatch(h2, dest, tail, *, n_rows, sub):
    n, d = h2.shape
    tm = min(sub, 128)
    return pl.pallas_call(
        _dispatch_kernel,
        out_shape=jax.ShapeDtypeStruct((n_rows, d), h2.dtype),
        grid=(n // tm,),
        in_specs=[pl.BlockSpec((1, 1, tm * TOP_K), lambda i: (i, 0, 0), memory_space=pltpu.SMEM),
                  pl.BlockSpec(memory_space=pltpu.SMEM),
                  pl.BlockSpec((tm, d), lambda i: (i, 0))],
        out_specs=pl.BlockSpec(memory_space=pl.ANY),
        scratch_shapes=[pltpu.VMEM((MOE_SUB, d), h2.dtype), pltpu.SemaphoreType.DMA(()), pltpu.SemaphoreType.DMA(())],
        compiler_params=_cparams(("arbitrary",)),
        name="moe_dispatch",
    )(dest.reshape(n // tm, 1, tm * TOP_K), tail.reshape(1, N_EXPERTS), h2)


def _combine_kernel(idx_ref, x_ref, g_ref, mod_ref, fw_ref, yb_ref, o_ref, buf, sem, *,
                    sub, subs_per_batch, n_batch, final):
    tm = x_ref.shape[0]

    def row_copy(t, k, row):
        return pltpu.make_async_copy(yb_ref.at[pl.ds(row, 1), :], buf.at[k, pl.ds(t, 1), :], sem)

    def issue(t, carry):
        for k in range(TOP_K):
            row_copy(t, k, idx_ref[0, 0, t * TOP_K + k]).start()
        return carry

    def drain(t, carry):
        for k in range(TOP_K):
            row_copy(t, k, 0).wait()
        return carry

    lax.fori_loop(0, tm, issue, 0)
    lax.fori_loop(0, tm, drain, 0)
    g = g_ref[...]
    y = g[:, 0:1] * buf[0]
    for k in range(1, TOP_K):
        y = y + g[:, k:k + 1] * buf[k]
    m = mod_ref[_group_of((pl.program_id(0) * tm) // sub, subs_per_batch, n_batch)]
    x_new = x_ref[...] + m[5:6] * y
    o_ref[...] = _rms(x_new, NORM_EPS) * fw_ref[...] if final else x_new


def _combine(xt, yb, dest, top_g, mod, final_w, *, sub, subs_per_batch, n_batch, final):
    n, d = xt.shape
    tm = min(sub, 128)
    kern = functools.partial(_combine_kernel, sub=sub, subs_per_batch=subs_per_batch, n_batch=n_batch, final=final)
    row = pl.BlockSpec((tm, d), lambda i: (i, 0))
    return pl.pallas_call(
        kern,
        out_shape=jax.ShapeDtypeStruct((n, d), F32),
        grid=(n // tm,),
        in_specs=[pl.BlockSpec((1, 1, tm * TOP_K), lambda i: (i, 0, 0), memory_space=pltpu.SMEM),
                  row, pl.BlockSpec((tm, LANES), lambda i: (i, 0)),
                  pl.BlockSpec((MOD_ROWS, 6, d), lambda i: (0, 0, 0)), pl.BlockSpec((1, d), lambda i: (0, 0)),
                  pl.BlockSpec(memory_space=pl.ANY)],
        out_specs=row,
        scratch_shapes=[pltpu.VMEM((TOP_K, tm, d), F32), pltpu.SemaphoreType.DMA(())],
        compiler_params=_cparams(("arbitrary",)),
        name="moe_combine",
    )(dest.reshape(n // tm, 1, tm * TOP_K), xt, top_g, mod, final_w.reshape(1, d), yb)


def _pad_cols(w, cuts):
    parts = []
    for start, width, padded in cuts:
        parts.append(w[:, start:start + width])
        if padded > width:
            parts.append(jnp.zeros((w.shape[0], padded - width), w.dtype))
    return jnp.concatenate(parts, axis=1)


def kernel(x, c, ctx, c_ctx, w_mod, b_mod, norm1_w, w_in, gla_gate_w, gla_gate_b, gla_norm_w, gdn_conv_w,
           gdn_a_log, gdn_dt_bias, gdn_norm_w, diff_lambda, diff_norm_w, w_out, norm2_w, router_w, router_b,
           expert_w1, expert_b1, expert_w2, expert_b2, final_norm_w):
    n_batch, s_len, d = x.shape
    l_ctx = ctx.shape[1]
    depth = w_mod.shape[0]
    t_len = l_ctx + s_len
    n_tok = n_batch * t_len
    subs_per_batch = t_len // l_ctx
    assert s_len % l_ctx == 0 and l_ctx % CHUNK == 0 and n_batch < MOD_ROWS and l_ctx % 16 == 0
    rows_per_visit = -(-(n_tok * TOP_K // N_EXPERTS * 9 // 8) // MOE_SUB) * MOE_SUB

    xt = jnp.concatenate([ctx, x], axis=1).reshape(n_tok, d)
    cvec = jnp.concatenate([c, c_ctx[None, :], jnp.zeros((MOD_ROWS - n_batch - 1, d), F32)], axis=0)
    mod_all = _modulation(cvec, w_mod, b_mod).reshape(depth, MOD_ROWS, 6, d)
    cos_t, sa_t, sb_t = _rope_tables(t_len, l_ctx)
    perm = _perm_matrix()
    a_w = GLA_HEADS * (2 * GLA_DK + 2 * GLA_DV)
    b_w = GDN_HEADS * (2 * GDN_DK + 2 * GDN_DV)
    c_w = DIFF_HEADS * (4 * DIFF_DH + DIFF_DV)
    glr0, b0 = a_w, a_w + 2 * GLA_RANK
    ba0 = b0 + b_w
    c0 = ba0 + 4 * GDN_HEADS
    cuts = ((0, a_w, a_w), (b0, b_w, b_w), (c0, c_w, c_w), (glr0, 2 * GLA_RANK, LANES), (ba0, 4 * GDN_HEADS, LANES))
    assert (a_w, a_w + b_w, a_w + b_w + c_w) == (B_Q, C_Q, A_GLR)

    for l in range(depth):
        lambda_init = 0.8 - 0.6 * math.exp(-0.3 * l)
        mod = mod_all[l]
        w_pad = _pad_cols(w_in[l], cuts).astype(BF16)
        p = _in_proj(xt, norm1_w[l], mod, w_pad, sub=l_ctx, subs_per_batch=subs_per_batch, n_batch=n_batch)

        gw = gla_gate_w[l]
        gw_pad = jnp.zeros((2, LANES, GLA_HEADS * GLA_DK), F32)
        gw_pad = gw_pad.at[0, 0:GLA_RANK].set(gw[0]).at[1, GLA_RANK:2 * GLA_RANK].set(gw[1])
        oa = _gla(p, gw_pad, gla_gate_b[l].reshape(2, 1, -1), n_batch=n_batch, t_len=t_len, l_ctx=l_ctx)

        ad = jnp.zeros((8, LANES), F32)
        ad = ad.at[0, 2 * GDN_HEADS:4 * GDN_HEADS].set(gdn_a_log[l].reshape(-1))
        ad = ad.at[1, 2 * GDN_HEADS:4 * GDN_HEADS].set(gdn_dt_bias[l].reshape(-1))
        ob = _gdn(p, gdn_conv_w[l], ad, n_batch=n_batch, t_len=t_len, l_ctx=l_ctx)

        qk = _rope(p, cos_t, sa_t, sb_t, tm=l_ctx)
        yd = _diff_attn(qk, p, diff_lambda[l], diff_norm_w[l], n_batch=n_batch, t_len=t_len, l_ctx=l_ctx,
                        lambda_init=lambda_init)

        rw_pad = jnp.zeros((d, LANES), F32).at[:, :N_EXPERTS].set(router_w[l])
        rb_pad = jnp.full((1, LANES), -1e30, F32).at[0, :N_EXPERTS].set(router_b[l])
        xt, h2, top_e, top_g = _out_proj(oa, ob, yd, p, gla_norm_w[l], gdn_norm_w[l], w_out[l].astype(BF16), xt, mod,
                                         norm2_w[l], rw_pad, rb_pad, tm=l_ctx, subs_per_batch=subs_per_batch,
                                         n_batch=n_batch)

        dest, tail, vis_e, vis_n, n_used, n_vis = _routing(top_e[:, :TOP_K], n_tok, rows_per_visit)
        xs = _dispatch(h2, dest, tail, n_rows=n_vis * rows_per_visit, sub=l_ctx)
        yb = _moe_experts(xs, vis_e, vis_n, n_used, expert_w1, _deinterleave_bias(expert_b1[l]),
                          expert_w2, expert_b2[l], perm, layer=l, rows_per_visit=rows_per_visit)
        xt = _combine(xt, yb, dest, top_g, mod, final_norm_w, sub=l_ctx, subs_per_batch=subs_per_batch,
                      n_batch=n_batch, final=(l == depth - 1))
    return xt.reshape(n_batch, t_len, d)[:, l_ctx:, :]
```

```python
import functools
import math

import jax
import jax.numpy as jnp
from jax import lax
from jax.experimental import pallas as pl
from jax.experimental.pallas import tpu as pltpu

F32 = jnp.float32
BF16 = jnp.bfloat16

GRID_W = 64
GLA_HEADS, GLA_DK, GLA_DV, GLA_RANK, GLA_GATE_NORM = 4, 64, 128, 16, 16.0
GDN_HEADS, GDN_DK, GDN_DV, GDN_CONV = 6, 128, 128, 5
DIFF_HEADS, DIFF_DH, DIFF_DV, DIFF_EPS = 6, 64, 128, 1e-5
ROPE_THETA = 10000.0
CHUNK = 64
N_EXPERTS, TOP_K = 32, 4
SWIGLU_ALPHA, SWIGLU_LIMIT = 1.702, 7.0
NORM_EPS = 1e-6
LANES = 128
MXU_N = 256
MOD_ROWS = 8

A_Q, A_K, A_V, A_OG = 0, 256, 512, 1024
B_Q, B_K, B_V, B_OG = 1536, 2304, 3072, 3840
C_Q, C_K, C_V = 4608, 5376, 6144
A_GLR, B_BA = 6912, 7040
NP_IN = 7168
D_MIX = GLA_HEADS * GLA_DV + GDN_HEADS * GDN_DV + DIFF_HEADS * DIFF_DV

VMEM_LIMIT = 56 * 1024 * 1024

def _dot(a, b):
    return jnp.dot(a, b, preferred_element_type=F32)


def _dot_nt(a, b):
    return lax.dot_general(a, b, (((1,), (1,)), ((), ())), preferred_element_type=F32)


def _split3(x):
    hi = x.astype(BF16)
    r = x - hi.astype(F32)
    mid = r.astype(BF16)
    return hi, mid, (r - mid.astype(F32)).astype(BF16)


def _dot_sel(sel, x, nt=False):
    dot = _dot_nt if nt else _dot
    s = sel.astype(F32).astype(BF16)
    hi, mid, lo = _split3(x)
    return dot(s, hi) + dot(s, mid) + dot(s, lo)


def _dot3(a, b):
    a_hi = a.astype(BF16)
    b_hi = b.astype(BF16)
    a_lo = (a - a_hi.astype(F32)).astype(BF16)
    b_lo = (b - b_hi.astype(F32)).astype(BF16)
    return _dot(a_hi, b_hi) + _dot(a_hi, b_lo) + _dot(a_lo, b_hi)


def _pick(n, pref):
    if n <= pref:
        return n
    t = pref - pref % LANES
    while t > LANES and n % t:
        t -= LANES
    return t


def _silu(x):
    return x * jax.nn.sigmoid(x)


def _log_sigmoid(x):
    return jnp.minimum(x, 0.0) - jnp.log(1.0 + jnp.exp(-jnp.abs(x)))


def _softplus(x):
    return jnp.maximum(x, 0.0) + jnp.log(1.0 + jnp.exp(-jnp.abs(x)))


def _cparams(sem):
    return pltpu.CompilerParams(dimension_semantics=sem, vmem_limit_bytes=VMEM_LIMIT)


def _mod_kernel(c_ref, w_ref, b_ref, o_ref):
    a = _silu(c_ref[...])
    o_ref[0] = _dot(a.astype(BF16), w_ref[0].astype(BF16)) + b_ref[0]


def _modulation(cvec, w_mod, b_mod):
    depth, d, n6 = w_mod.shape
    tn = _pick(n6, 1024)
    return pl.pallas_call(
        _mod_kernel,
        out_shape=jax.ShapeDtypeStruct((depth, MOD_ROWS, n6), F32),
        grid=(depth, n6 // tn),
        in_specs=[pl.BlockSpec((MOD_ROWS, d), lambda l, j: (0, 0)),
                  pl.BlockSpec((1, d, tn), lambda l, j: (l, 0, j)),
                  pl.BlockSpec((1, 1, tn), lambda l, j: (l, 0, j))],
        out_specs=pl.BlockSpec((1, MOD_ROWS, tn), lambda l, j: (l, 0, j)),
        compiler_params=_cparams(("parallel", "parallel")),
        name="modulation",
    )(cvec, w_mod, b_mod.reshape(depth, 1, n6))


def _group_of(sub_block, subs_per_batch, n_batch):
    return jnp.where(sub_block % subs_per_batch == 0, n_batch, sub_block // subs_per_batch)


def _rms(x, eps):
    return x * lax.rsqrt(jnp.mean(x * x, axis=-1, keepdims=True) + eps)


def _in_proj_kernel(x_ref, nw_ref, mod_ref, w_ref, o_ref, h_scr, *, sub, subs_per_batch, n_batch):
    i = pl.program_id(0)

    @pl.when(pl.program_id(1) == 0)
    def _():
        for s in range(x_ref.shape[0] // sub):
            m = mod_ref[_group_of(i * (x_ref.shape[0] // sub) + s, subs_per_batch, n_batch)]
            y = _rms(x_ref[s * sub:(s + 1) * sub, :], NORM_EPS) * nw_ref[...]
            h_scr[s * sub:(s + 1) * sub, :] = (y * (1.0 + m[1:2]) + m[0:1]).astype(BF16)

    o_ref[...] = _dot(h_scr[...], w_ref[...])


def _in_proj(xt, norm_w, mod, w_pad, *, sub, subs_per_batch, n_batch):
    n, d = xt.shape
    npad = w_pad.shape[1]
    tm = sub * 2 if (n // sub) % 2 == 0 else sub
    tn = _pick(npad, 1024)
    kern = functools.partial(_in_proj_kernel, sub=sub, subs_per_batch=subs_per_batch, n_batch=n_batch)
    return pl.pallas_call(
        kern,
        out_shape=jax.ShapeDtypeStruct((n, npad), F32),
        grid=(n // tm, npad // tn),
        in_specs=[pl.BlockSpec((tm, d), lambda i, j: (i, 0)),
                  pl.BlockSpec((1, d), lambda i, j: (0, 0)),
                  pl.BlockSpec((MOD_ROWS, 6, d), lambda i, j: (0, 0, 0)),
                  pl.BlockSpec((d, tn), lambda i, j: (0, j))],
        out_specs=pl.BlockSpec((tm, tn), lambda i, j: (i, j)),
        scratch_shapes=[pltpu.VMEM((tm, d), BF16)],
        compiler_params=_cparams(("parallel", "arbitrary")),
        name="in_proj",
    )(xt, norm_w.reshape(1, d), mod, w_pad)


def _tri_masks(c):
    r = lax.broadcasted_iota(jnp.int32, (c, c), 0)
    s = lax.broadcasted_iota(jnp.int32, (c, c), 1)
    return ((s <= r, s < r), (s >= r, s > r))


def _bwd_chunk(i, nc_ctx, nc):
    return jnp.where(i < nc_ctx, nc_ctx - 1 - i, nc - 1 - (i - nc_ctx))


def _gla_kernel(q_ref, k_ref, v_ref, glr_ref, gw_ref, gb_ref, o_ref, st_scr, *, nc_ctx, group):
    t_len = q_ref.shape[0]
    nc = t_len // CHUNK
    masks = _tri_masks(CHUNK)
    lane = lax.broadcasted_iota(jnp.int32, (1, LANES), 1)
    head_mask = [(lane < GLA_DK).astype(F32), (lane >= GLA_DK).astype(F32)]
    o_ref[...] = jnp.zeros(o_ref.shape, F32)
    st_scr[...] = jnp.zeros(st_scr.shape, F32)

    def step(gi, carry):
        zu = [(z, u) for z in range(2) for u in range(group)]
        cs = [gi * group + u if z == 0 else _bwd_chunk(gi * group + u, nc_ctx, nc) for z, u in zu]
        rows = [pl.ds(pl.multiple_of(c * CHUNK, CHUNK), CHUNK) for c in cs]
        pre = [_dot3(glr_ref[rows[n], :], gw_ref[z]) + gb_ref[z] for n, (z, u) in enumerate(zu)]
        b = [_dot_sel(masks[z][0], _log_sigmoid(pre[n]) / GLA_GATE_NORM) for n, (z, u) in enumerate(zu)]
        tot = [b[n][CHUNK - 1:CHUNK, :] if z == 0 else b[n][0:1, :] for n, (z, u) in enumerate(zu)]
        qd, kk, kd = [], [], []
        for n in range(len(zu)):
            k = k_ref[rows[n], :]
            qd.append(q_ref[rows[n], :] * GLA_DK ** -0.5 * jnp.exp(b[n]))
            kk.append((k * jnp.exp(-b[n])).astype(BF16))
            kd.append(k * jnp.exp(tot[n] - b[n]))
        nj = [(n, j) for n in range(len(zu)) for j in range(2)]
        qdj = [(qd[n] * head_mask[j]).astype(BF16) for n, j in nj]
        vj = [v_ref[rows[n], j * GLA_DV:(j + 1) * GLA_DV] for n, j in nj]
        att = [jnp.where(masks[zu[n][0]][0], _dot_nt(qdj[m], kk[n]), 0.0).astype(BF16) for m, (n, j) in enumerate(nj)]
        intra = [_dot(att[m], vj[m].astype(BF16)) for m in range(len(nj))]
        upd = [_dot(vj[m].T.astype(BF16), (kd[n] * head_mask[j]).astype(BF16)) for m, (n, j) in enumerate(nj)]
        for z in range(2):
            for j in range(2):
                st = st_scr[z, j]
                for u in range(group):
                    n = z * group + u
                    m = 2 * n + j
                    o_ref[rows[n], j * GLA_DV:(j + 1) * GLA_DV] += intra[m] + _dot_nt(qdj[m], st.astype(BF16))
                    st = st * jnp.exp(tot[n]) + upd[m]
                st_scr[z, j] = st
        return carry

    lax.fori_loop(0, nc // group, step, 0)


def _gla(p, gw_pad, gb, *, n_batch, t_len, l_ctx):
    n = p.shape[0]
    spec = lambda col, w: pl.BlockSpec((t_len, w), lambda b, hp, col=col, w=w: (b, col // w + hp))
    group = next(g for g in (4, 2, 1) if (t_len // CHUNK) % g == 0)
    kern = functools.partial(_gla_kernel, nc_ctx=l_ctx // CHUNK, group=group)
    return pl.pallas_call(
        kern,
        out_shape=jax.ShapeDtypeStruct((n, GLA_HEADS * GLA_DV), F32),
        grid=(n_batch, GLA_HEADS // 2),
        in_specs=[spec(A_Q, LANES), spec(A_K, LANES), spec(A_V, 2 * GLA_DV),
                  pl.BlockSpec((t_len, LANES), lambda b, hp: (b, A_GLR // LANES)),
                  pl.BlockSpec((2, LANES, LANES), lambda b, hp: (0, 0, hp)),
                  pl.BlockSpec((2, 1, LANES), lambda b, hp: (0, 0, hp))],
        out_specs=pl.BlockSpec((t_len, 2 * GLA_DV), lambda b, hp: (b, hp)),
        scratch_shapes=[pltpu.VMEM((2, 2, GLA_DV, LANES), F32)],
        compiler_params=_cparams(("parallel", "parallel")),
        name="gla_scan",
    )(p, p, p, p, gw_pad, gb)


def _gdn_kernel(q_ref, k_ref, v_ref, ba_ref, cwq_ref, cwk_ref, cwv_ref, ad_ref, o_ref,
                m_scr, b_scr, qp_scr, dec_scr, s_scr, *, nc_ctx, l_ctx, group):
    t_len = q_ref.shape[0]
    nc = t_len // CHUNK
    h = pl.program_id(1)
    masks = _tri_masks(CHUNK)
    lane = lax.broadcasted_iota(jnp.int32, (1, LANES), 1)
    halo = 8
    pad = (GDN_CONV - 1) // 2
    ones_l0 = jnp.broadcast_to((lane == 0).astype(F32), (CHUNK, LANES))

    def tap_masks(r0):
        row = r0 + lax.broadcasted_iota(jnp.int32, (CHUNK, LANES), 0)
        return [(((row + d) >= l_ctx) == (row >= l_ctx)) & (row + d >= 0) & (row + d < t_len)
                for d in range(-pad, pad + 1)]

    def conv(src_ref, cw_ref, r0, ok):
        top = src_ref[pl.ds(pl.multiple_of(jnp.maximum(r0 - halo, 0), halo), halo), :]
        bot = src_ref[pl.ds(pl.multiple_of(jnp.minimum(r0 + CHUNK, t_len - halo), halo), halo), :]
        win = jnp.concatenate([top, src_ref[pl.ds(r0, CHUNK), :], bot], axis=0)
        acc = jnp.zeros((CHUNK, LANES), F32)
        for j in range(GDN_CONV):
            tap = win[halo + j - pad:halo + j - pad + CHUNK, :]
            acc = acc + jnp.where(ok[j], tap, 0.0) * cw_ref[j:j + 1, :]
        return _silu(acc)

    def l2n(y):
        return y * lax.rsqrt(jnp.sum(y * y, axis=-1, keepdims=True) + 1e-6)

    def prep(gi, carry):
        per_chunk, chains = [], []
        for cc in range(group):
            c = gi * group + cc
            r0 = pl.multiple_of(c * CHUNK, CHUNK)
            ok = tap_masks(r0)
            q = l2n(conv(q_ref, cwq_ref, r0, ok)) * GDN_DK ** -0.5
            k = l2n(conv(k_ref, cwk_ref, r0, ok))
            v = conv(v_ref, cwv_ref, r0, ok)
            ba = ba_ref[pl.ds(r0, CHUNK), :]
            beta_all = jax.nn.sigmoid(ba)
            g_all = -jnp.exp(ad_ref[0:1, :]) * _softplus(ba + ad_ref[1:2, :])
            per_chunk.append((c, r0, q, k, v, beta_all, g_all))
        grams = []
        for (_, _, q, k, _, _, _) in per_chunk:
            kbf = k.astype(BF16)
            grams.append((_dot_nt(kbf, kbf), _dot_nt(q.astype(BF16), kbf)))
        gcs = []
        for (_, _, _, _, _, beta_all, g_all) in per_chunk:
            for z in range(2):
                col = z * GDN_HEADS + h
                beta = jnp.sum(jnp.where(lane == col, beta_all, 0.0), axis=-1, keepdims=True)
                g = jnp.sum(jnp.where(lane == 2 * GDN_HEADS + col, g_all, 0.0), axis=-1, keepdims=True)
                gcs.append((beta, _dot_sel(masks[z][0], jnp.broadcast_to(g, (CHUNK, LANES)))))
        grows = [_dot_sel(ones_l0, gc, nt=True) for (_, gc) in gcs]
        xs, npows, aqk, qg, kdt, where = [], [], [], [], [], []
        for ci, (c, r0, q, k, v, _, _) in enumerate(per_chunk):
            kk, qk = grams[ci]
            for z in range(2):
                incl, strict = masks[z]
                beta, gc = gcs[2 * ci + z]
                dmat = jnp.exp(jnp.where(incl, gc[:, :CHUNK] - grows[2 * ci + z], -jnp.inf))
                tot = gc[CHUNK - 1:CHUNK, :] if z == 0 else gc[0:1, :]
                where.append((z, c, r0))
                aqk.append(jnp.where(incl, qk * dmat, 0.0).astype(BF16))
                qg.append(q * jnp.exp(gc))
                kdt.append((k * jnp.exp(tot - gc)).T.astype(BF16))
                dec_scr[z, pl.ds(pl.multiple_of(c * halo, halo), halo), :] = jnp.broadcast_to(jnp.exp(tot), (halo, LANES))
                npows.append(-jnp.where(strict, beta * kk * dmat, 0.0))
                xs.append(jnp.concatenate([v * beta, k * (beta * jnp.exp(gc))], axis=1))
        for lvl in range(6):
            xs = [x + _dot3(npw, x) for x, npw in zip(xs, npows)]
            if lvl < 5:
                npows = [_dot3(npw, npw) for npw in npows]
        ub = [x[:, :GDN_DV].astype(BF16) for x in xs]
        wb = [x[:, GDN_DV:].astype(BF16) for x in xs]
        m_all = [_dot(kdt[n], wb[n]) for n in range(len(xs))]
        b_all = [_dot(kdt[n], ub[n]) for n in range(len(xs))]
        aw = [_dot(aqk[n], wb[n]) for n in range(len(xs))]
        au = [_dot(aqk[n], ub[n]) for n in range(len(xs))]
        for n, (z, c, r0) in enumerate(where):
            blk = pl.ds(pl.multiple_of(c * GDN_DK, GDN_DK), GDN_DK)
            m_scr[z, blk, :] = (-m_all[n]).astype(BF16)
            b_scr[z, blk, :] = b_all[n]
            qp_scr[z, pl.ds(r0, CHUNK), :] = (qg[n] - aw[n]).astype(BF16)
            o_ref[pl.ds(r0, CHUNK), :] += au[n]
        return carry

    o_ref[...] = jnp.zeros(o_ref.shape, F32)
    lax.fori_loop(0, nc // group, prep, 0)
    s_scr[...] = jnp.zeros(s_scr.shape, F32)

    def step(i, carry):
        cs = (i, _bwd_chunk(i, nc_ctx, nc))
        s = [s_scr[z] for z in range(2)]
        sb = [s[z].astype(BF16) for z in range(2)]
        blk = [pl.ds(pl.multiple_of(cs[z] * GDN_DK, GDN_DK), GDN_DK) for z in range(2)]
        rows = [pl.ds(pl.multiple_of(cs[z] * CHUNK, CHUNK), CHUNK) for z in range(2)]
        ms = [_dot(m_scr[z, blk[z], :], sb[z]) for z in range(2)]
        qs = [_dot(qp_scr[z, rows[z], :], sb[z]) for z in range(2)]
        for z in range(2):
            dec = dec_scr[z, pl.ds(pl.multiple_of(cs[z] * halo, halo), 1), :]
            s_scr[z] = s[z] * dec + ms[z] + b_scr[z, blk[z], :]
            o_ref[rows[z], :] += qs[z]
        return carry

    lax.fori_loop(0, nc, step, 0)


def _gdn(p, conv_w, ad, *, n_batch, t_len, l_ctx):
    n = p.shape[0]
    spec = lambda col: pl.BlockSpec((t_len, LANES), lambda b, h, col=col: (b, col // LANES + h))
    cspec = lambda off: pl.BlockSpec((GDN_CONV, LANES), lambda b, h, off=off: (0, off + h))
    nc = t_len // CHUNK
    group = next(g for g in (4, 2, 1) if nc % g == 0)
    kern = functools.partial(_gdn_kernel, nc_ctx=l_ctx // CHUNK, l_ctx=l_ctx, group=group)
    return pl.pallas_call(
        kern,
        out_shape=jax.ShapeDtypeStruct((n, GDN_HEADS * GDN_DV), F32),
        grid=(n_batch, GDN_HEADS),
        in_specs=[spec(B_Q), spec(B_K), spec(B_V),
                  pl.BlockSpec((t_len, LANES), lambda b, h: (b, B_BA // LANES)),
                  cspec(0), cspec(GDN_HEADS), cspec(2 * GDN_HEADS),
                  pl.BlockSpec((8, LANES), lambda b, h: (0, 0))],
        out_specs=pl.BlockSpec((t_len, GDN_DV), lambda b, h: (b, h)),
        scratch_shapes=[pltpu.VMEM((2, nc * GDN_DK, GDN_DK), BF16),
                        pltpu.VMEM((2, nc * GDN_DK, GDN_DV), F32),
                        pltpu.VMEM((2, t_len, GDN_DK), BF16),
                        pltpu.VMEM((2, nc * 8, LANES), F32),
                        pltpu.VMEM((2, GDN_DK, GDN_DV), F32)],
        compiler_params=_cparams(("parallel", "parallel")),
        name="gdn_scan",
    )(p, p, p, p, conv_w, conv_w, conv_w, ad)


def _rope_kernel(x_ref, cos_ref, sa_ref, sb_ref, o_ref):
    scale = jnp.where(pl.program_id(1) == 0, DIFF_DH ** -0.5, 1.0)
    cos, sa, sb = cos_ref[...] * scale, sa_ref[...] * scale, sb_ref[...] * scale
    for hh in range(DIFF_HEADS):
        x = x_ref[:, hh * LANES:(hh + 1) * LANES]
        y = x * cos + pltpu.roll(x, LANES - 16, 1) * sa + pltpu.roll(x, 16, 1) * sb
        o_ref[:, hh * LANES:(hh + 1) * LANES] = y.astype(BF16)


def _rope(p, cos_t, sa_t, sb_t, *, tm):
    n = p.shape[0]
    tblocks = cos_t.shape[0] // tm
    wq = DIFF_HEADS * LANES
    tspec = pl.BlockSpec((tm, LANES), lambda i, j: (i % tblocks, 0))
    return pl.pallas_call(
        _rope_kernel,
        out_shape=jax.ShapeDtypeStruct((n, 2 * wq), BF16),
        grid=(n // tm, 2),
        in_specs=[pl.BlockSpec((tm, wq), lambda i, j: (i, C_Q // wq + j)), tspec, tspec, tspec],
        out_specs=pl.BlockSpec((tm, wq), lambda i, j: (i, j)),
        compiler_params=_cparams(("parallel", "parallel")),
        name="rope",
    )(p, cos_t, sa_t, sb_t)


def _rope_tables(t_len, l_ctx):
    pos = jnp.arange(t_len - l_ctx)
    row = (pos // GRID_W).astype(F32)
    colp = (pos % GRID_W).astype(F32)
    half = DIFF_DH // 2
    inv = 1.0 / (ROPE_THETA ** (jnp.arange(0, half, 2, dtype=F32) / half))
    lane = jnp.arange(LANES)
    part = (lane % DIFF_DH) // half
    upper = (lane % half) // (half // 2)
    ang = jnp.where(part[None, :] == 0, row[:, None], colp[:, None]) * inv[lane % (half // 2)][None, :]
    cos, sin = jnp.cos(ang), jnp.sin(ang)
    sa = jnp.where(upper[None, :] == 0, -sin, 0.0)
    sb = jnp.where(upper[None, :] == 1, sin, 0.0)
    ident = lambda v, fill: jnp.concatenate([jnp.full((l_ctx, LANES), fill, F32), v.astype(F32)], axis=0)
    return ident(cos, 1.0), ident(sa, 0.0), ident(sb, 0.0)


def _attn_kernel(q_ref, k_ref, v_ref, lam_ref, nw_ref, o_ref, *, l_ctx, lambda_init):
    tq = q_ref.shape[0]
    t_len = k_ref.shape[0]
    lv = lam_ref[...]
    lam = (jnp.exp(jnp.sum(lv[0:1] * lv[1:2], axis=-1, keepdims=True))
           - jnp.exp(jnp.sum(lv[2:3] * lv[3:4], axis=-1, keepdims=True)) + lambda_init)
    lane = lax.broadcasted_iota(jnp.int32, (1, LANES), 1)

    def attend(kv_len):
        q = q_ref[...]
        k = k_ref[0:kv_len, :]
        v = v_ref[0:kv_len, :].astype(BF16)
        es, inv = [], []
        for m in range(2):
            qm = jnp.where((lane >= DIFF_DH) == (m == 1), q, jnp.zeros_like(q))
            s = _dot_nt(qm, k)
            e = jnp.exp(s - jnp.max(s, axis=-1, keepdims=True))
            es.append(e)
            inv.append(1.0 / jnp.sum(e, axis=-1, keepdims=True))
        w = es[0] * inv[0] - es[1] * (lam * inv[1])
        o = _dot(w.astype(BF16), v)
        o_ref[...] = _rms(o, DIFF_EPS) * nw_ref[...] * (1.0 - lambda_init)

    is_ctx = pl.program_id(2) < l_ctx // tq
    pl.when(is_ctx)(lambda: attend(l_ctx))
    pl.when(jnp.logical_not(is_ctx))(lambda: attend(t_len))


def _diff_attn(qk, p, lam_vecs, norm_w, *, n_batch, t_len, l_ctx, lambda_init):
    n = p.shape[0]
    tq = min(l_ctx, 256)
    nt = t_len // tq
    kern = functools.partial(_attn_kernel, l_ctx=l_ctx, lambda_init=lambda_init)
    return pl.pallas_call(
        kern,
        out_shape=jax.ShapeDtypeStruct((n, DIFF_HEADS * DIFF_DV), F32),
        grid=(n_batch, DIFF_HEADS, nt),
        in_specs=[pl.BlockSpec((tq, LANES), lambda b, h, t: (b * nt + t, h)),
                  pl.BlockSpec((t_len, LANES), lambda b, h, t: (b, DIFF_HEADS + h)),
                  pl.BlockSpec((t_len, LANES), lambda b, h, t: (b, C_V // LANES + h)),
                  pl.BlockSpec((4, DIFF_DH), lambda b, h, t: (0, 0)),
                  pl.BlockSpec((1, DIFF_DV), lambda b, h, t: (0, 0))],
        out_specs=pl.BlockSpec((tq, DIFF_DV), lambda b, h, t: (b * nt + t, h)),
        compiler_params=_cparams(("parallel", "parallel", "parallel")),
        name="diff_attn",
    )(qk, qk, p, lam_vecs, norm_w.reshape(1, DIFF_DV))


def _out_proj_kernel(oa_ref, oga_ref, ob_ref, ogb_ref, yd_ref, anw_ref, bnw_ref, w_ref, x_ref, mod_ref,
                     n2_ref, rw_ref, rb_ref, xo_ref, h2_ref, te_ref, tg_ref, *, subs_per_batch, n_batch):
    m = mod_ref[_group_of(pl.program_id(0), subs_per_batch, n_batch)]
    pieces = []
    for o_ref, og_ref, nw_ref, heads in ((oa_ref, oga_ref, anw_ref, GLA_HEADS), (ob_ref, ogb_ref, bnw_ref, GDN_HEADS)):
        for hh in range(heads):
            sl = slice(hh * LANES, (hh + 1) * LANES)
            pieces.append((_rms(o_ref[:, sl], NORM_EPS) * nw_ref[...] * _silu(og_ref[:, sl])).astype(BF16))
    pieces.append(yd_ref[...].astype(BF16))
    mix = jnp.concatenate(pieces, axis=1)
    x_new = x_ref[...] + m[2:3] * _dot(mix, w_ref[...])
    xo_ref[...] = x_new
    h2 = _rms(x_new, NORM_EPS) * n2_ref[...] * (1.0 + m[4:5]) + m[3:4]
    bits = lax.bitcast_convert_type(h2.astype(BF16).astype(F32), jnp.uint32)
    half = bits.shape[1] // 2
    h2_ref[...] = (bits[:, :half] >> 16) | (bits[:, half:] & jnp.uint32(0xFFFF0000))
    lg = _dot3(h2, rw_ref[...]) + rb_ref[...]
    lane = lax.broadcasted_iota(jnp.int32, lg.shape, 1)
    lane_f = lane.astype(F32)
    vals, idxs = [], []
    for _ in range(TOP_K):
        mx = jnp.max(lg, axis=-1, keepdims=True)
        ix = jnp.min(jnp.where(lg == mx, lane_f, float(LANES)), axis=-1, keepdims=True)
        vals.append(mx)
        idxs.append(ix)
        lg = jnp.where(lane_f == ix, -jnp.inf, lg)
    ex = [jnp.exp(v - vals[0]) for v in vals]
    inv_den = 1.0 / (ex[0] + ex[1] + ex[2] + ex[3])
    te = jnp.zeros(lg.shape, F32)
    tg = jnp.zeros(lg.shape, F32)
    for kk in range(TOP_K):
        te = jnp.where(lane == kk, idxs[kk], te)
        tg = jnp.where(lane == kk, ex[kk] * inv_den, tg)
    te_ref[...] = te.astype(jnp.int32)
    tg_ref[...] = tg


def _out_proj(oa, ob, yd, p, gla_nw, gdn_nw, w_out, xt, mod, norm2_w, rw_pad, rb_pad, *, tm, subs_per_batch, n_batch):
    n, d = xt.shape
    kern = functools.partial(_out_proj_kernel, subs_per_batch=subs_per_batch, n_batch=n_batch)
    const = lambda shape: pl.BlockSpec(shape, lambda i: (0,) * len(shape))
    wa, wb, wd = GLA_HEADS * GLA_DV, GDN_HEADS * GDN_DV, DIFF_HEADS * DIFF_DV
    return pl.pallas_call(
        kern,
        out_shape=(jax.ShapeDtypeStruct((n, d), F32), jax.ShapeDtypeStruct((n, d // 2), jnp.uint32),
                   jax.ShapeDtypeStruct((n, LANES), jnp.int32), jax.ShapeDtypeStruct((n, LANES), F32)),
        grid=(n // tm,),
        in_specs=[pl.BlockSpec((tm, wa), lambda i: (i, 0)),
                  pl.BlockSpec((tm, wa), lambda i: (i, A_OG // wa)),
                  pl.BlockSpec((tm, wb), lambda i: (i, 0)),
                  pl.BlockSpec((tm, wb), lambda i: (i, B_OG // wb)),
                  pl.BlockSpec((tm, wd), lambda i: (i, 0)),
                  const((1, LANES)), const((1, LANES)), const((D_MIX, d)),
                  pl.BlockSpec((tm, d), lambda i: (i, 0)),
                  const((MOD_ROWS, 6, d)), const((1, d)), const((d, LANES)), const((1, LANES))],
        out_specs=(pl.BlockSpec((tm, d), lambda i: (i, 0)), pl.BlockSpec((tm, d // 2), lambda i: (i, 0)),
                   pl.BlockSpec((tm, LANES), lambda i: (i, 0)), pl.BlockSpec((tm, LANES), lambda i: (i, 0))),
        compiler_params=_cparams(("parallel",)),
        name="out_proj",
    )(oa, p, ob, p, yd, gla_nw.reshape(1, LANES), gdn_nw.reshape(1, LANES), w_out, xt, mod,
      norm2_w.reshape(1, d), rw_pad, rb_pad)


MOE_SUB = 256
PERM = 256


MOE_UNROLL = 4


def _for_sub_blocks(code, blocks):
    n = code // 2
    full = lambda first, count: blocks(
        [pl.ds(pl.multiple_of((first + u) * MOE_SUB, MOE_SUB), MOE_SUB) for u in range(count)])
    groups = n // MOE_UNROLL

    def body(i, carry):
        full(i * MOE_UNROLL, MOE_UNROLL)
        return carry

    lax.fori_loop(0, groups, body, 0)
    done = groups * MOE_UNROLL
    step = MOE_UNROLL // 2
    while step:
        take = ((n - done) & step) != 0
        pl.when(take)(functools.partial(full, done, step))
        done = done + jnp.where(take, step, 0)
        step //= 2
    pl.when(code % 2 == 1)(
        lambda: blocks([pl.ds(pl.multiple_of(n * MOE_SUB, MOE_SUB // 2), MOE_SUB // 2)]))


def _unpack_rows(u):
    lo = lax.bitcast_convert_type(u << 16, F32).astype(BF16)
    hi = lax.bitcast_convert_type(u & jnp.uint32(0xFFFF0000), F32).astype(BF16)
    return jnp.concatenate([lo, hi], axis=1)


def _moe_kernel(ve_ref, vn_ref, nu_ref, x_ref, w1_ref, b1_ref, perm_ref, w2_ref, b2_ref, o_ref, wp_scr, w2b_scr):
    v = pl.program_id(0)
    tn = w1_ref.shape[3]
    d = o_ref.shape[1]
    live = vn_ref[v] > 0

    @pl.when(live)
    def _():
        for c in range(tn // PERM):
            wc = w1_ref[0, 0, :, c * PERM:(c + 1) * PERM].astype(BF16)
            wp_scr[:, c * PERM:(c + 1) * PERM] = _dot(wc, perm_ref[...]).astype(BF16)
        w2b_scr[...] = w2_ref[0, 0].astype(BF16)

    @pl.when(live & (pl.program_id(1) == 0))
    def _():
        o_ref[...] = jnp.broadcast_to(b2_ref[0], o_ref.shape)

    def blocks(rows):
        hids = [_dot(_unpack_rows(x_ref[r, :]), wp_scr[...]) for r in rows]
        acts = []
        for hid in hids:
            hid = hid + b1_ref[0]
            parts = []
            for c in range(tn // PERM):
                glu = jnp.minimum(hid[:, c * PERM:c * PERM + PERM // 2], SWIGLU_LIMIT)
                lin = jnp.clip(hid[:, c * PERM + PERM // 2:(c + 1) * PERM], -SWIGLU_LIMIT, SWIGLU_LIMIT)
                parts.append((glu * jax.nn.sigmoid(SWIGLU_ALPHA * glu) * (lin + 1.0)).astype(BF16))
            acts.append(jnp.concatenate(parts, axis=1))
        for nt in range(d // MXU_N if d >= MXU_N else 1):
            cols = slice(nt * MXU_N, min((nt + 1) * MXU_N, d))
            ys = [_dot(a, w2b_scr[:, cols]) for a in acts]
            for r, y in zip(rows, ys):
                o_ref[r, cols] += y

    _for_sub_blocks(vn_ref[v], blocks)


def _moe_experts(xs, vis_e, vis_n, n_used, w1, b1p, w2, b2, perm, *, layer, rows_per_visit):
    _, n_exp, d, dh2 = w1.shape
    r = rows_per_visit
    n_vis = xs.shape[0] // r
    tn = _pick(dh2, 512)
    n_j = dh2 // tn
    live = lambda v, nu: v < nu[0]
    jj = lambda v, j, nu: jnp.where(live(v, nu), j, n_j - 1)
    return pl.pallas_call(
        _moe_kernel,
        out_shape=jax.ShapeDtypeStruct(((n_vis + 1) * r, d), F32),
        grid_spec=pltpu.PrefetchScalarGridSpec(
            num_scalar_prefetch=3, grid=(n_vis, n_j),
            in_specs=[pl.BlockSpec((r, d // 2), lambda v, j, ve, vn, nu: (jnp.minimum(v, nu[0] - 1), 0)),
                      pl.BlockSpec((1, 1, d, tn), lambda v, j, ve, vn, nu: (layer, ve[v], 0, jj(v, j, nu))),
                      pl.BlockSpec((1, 1, tn), lambda v, j, ve, vn, nu: (ve[v], 0, jj(v, j, nu))),
                      pl.BlockSpec((PERM, PERM), lambda v, j, ve, vn, nu: (0, 0)),
                      pl.BlockSpec((1, 1, tn // 2, d), lambda v, j, ve, vn, nu: (layer, ve[v], jj(v, j, nu), 0)),
                      pl.BlockSpec((1, 1, d), lambda v, j, ve, vn, nu: (ve[v], 0, 0))],
            out_specs=pl.BlockSpec((r, d), lambda v, j, ve, vn, nu: (jnp.where(live(v, nu), v, n_vis), 0)),
            scratch_shapes=[pltpu.VMEM((d, tn), BF16), pltpu.VMEM((tn // 2, d), BF16)]),
        compiler_params=_cparams(("arbitrary", "arbitrary")),
        name="moe_experts",
    )(vis_e, vis_n, n_used, xs, w1, b1p, perm, w2, b2.reshape(n_exp, 1, d))


def _perm_matrix():
    src = jnp.arange(PERM)
    dst = jnp.where(src % 2 == 0, src // 2, PERM // 2 + src // 2)
    return (dst[:, None] == jnp.arange(PERM)[None, :]).astype(BF16)


def _deinterleave_bias(b1):
    n_exp, dh2 = b1.shape
    b = b1.reshape(n_exp, dh2 // PERM, PERM // 2, 2)
    return jnp.concatenate([b[..., 0], b[..., 1]], axis=-1).reshape(n_exp, 1, dh2)


def _routing(top_e, n_tok, rows_per_visit):
    r = rows_per_visit
    n_assign = n_tok * TOP_K
    n_vis = n_assign // r + N_EXPERTS
    flat_e = top_e.reshape(-1)
    onehot = (flat_e[:, None] == jnp.arange(N_EXPERTS)[None, :]).astype(jnp.int32)
    rank = jnp.take_along_axis(jnp.cumsum(onehot, axis=0) - onehot, flat_e[:, None], axis=1)[:, 0]
    counts = jnp.sum(onehot, axis=0)
    seg_vis = (counts + r - 1) // r
    vis_end = jnp.cumsum(seg_vis)
    vis_start = vis_end - seg_vis
    dest = vis_start[flat_e] * r + rank
    v = jnp.arange(n_vis)
    n_used = vis_end[-1]
    vc = jnp.minimum(v, n_used - 1)
    vis_e = jnp.minimum(jnp.searchsorted(vis_end, vc, side="right"), N_EXPERTS - 1).astype(jnp.int32)
    valid = jnp.clip(counts[vis_e] - (vc - vis_start[vis_e]) * r, 0, r)
    left = valid % MOE_SUB
    n_full = valid // MOE_SUB + (left > MOE_SUB // 2)
    half = (left > 0) & (left <= MOE_SUB // 2)
    vis_n = jnp.where(v < n_used, 2 * n_full + half, 0).astype(jnp.int32)
    tail = jnp.minimum(vis_start * r + counts // MOE_SUB * MOE_SUB, (n_vis * r) - MOE_SUB).astype(jnp.int32)
    return dest.astype(jnp.int32), tail, vis_e, vis_n, n_used.reshape(1).astype(jnp.int32), n_vis


def _dispatch_kernel(idx_ref, tail_ref, h_ref, xs_ref, zero_scr, zsem, sem):
    tm = h_ref.shape[0]

    def zero_copy(e):
        start = pl.multiple_of(tail_ref[0, e], MOE_SUB)
        return pltpu.make_async_copy(zero_scr, xs_ref.at[pl.ds(start, MOE_SUB), :], zsem)

    @pl.when(pl.program_id(0) == 0)
    def _():
        zero_scr[...] = jnp.zeros(zero_scr.shape, zero_scr.dtype)
        for e in range(N_EXPERTS):
            zero_copy(e).start()
        for e in range(N_EXPERTS):
            zero_copy(e).wait()

    def row_copy(t, k, row):
        return pltpu.make_async_copy(h_ref.at[pl.ds(t, 1), :], xs_ref.at[pl.ds(row, 1), :], sem)

    def issue(t, carry):
        for k in range(TOP_K):
            row_copy(t, k, idx_ref[0, 0, t * TOP_K + k]).start(priority=k % 2)
        return carry

    def drain(t, carry):
        for k in range(TOP_K):
            row_copy(t, k, 0).wait()
        return carry

    lax.fori_loop(0, tm, issue, 0)
    lax.fori_loop(0, tm, drain, 0)


def _dispatch(h2, dest, tail, *, n_rows, sub):
    n, d = h2.shape
    tm = min(sub, 128)
    return pl.pallas_call(
        _dispatch_kernel,
        out_shape=jax.ShapeDtypeStruct((n_rows, d), h2.dtype),
        grid=(n // tm,),
        in_specs=[pl.BlockSpec((1, 1, tm * TOP_K), lambda i: (i, 0, 0), memory_space=pltpu.SMEM),
                  pl.BlockSpec(memory_space=pltpu.SMEM),
                  pl.BlockSpec((tm, d), lambda i: (i, 0))],
        out_specs=pl.BlockSpec(memory_space=pl.ANY),
        scratch_shapes=[pltpu.VMEM((MOE_SUB, d), h2.dtype), pltpu.SemaphoreType.DMA(()), pltpu.SemaphoreType.DMA(())],
        compiler_params=_cparams(("arbitrary",)),
        name="moe_dispatch",
    )(dest.reshape(n // tm, 1, tm * TOP_K), tail.reshape(1, N_EXPERTS), h2)


def _combine_kernel(idx_ref, x_ref, g_ref, mod_ref, fw_ref, yb_ref, o_ref, buf, sem, *,
                    sub, subs_per_batch, n_batch, final):
    tm = x_ref.shape[0]

    def row_copy(t, k, row):
        return pltpu.make_async_copy(yb_ref.at[pl.ds(row, 1), :], buf.at[k, pl.ds(t, 1), :], sem)

    def issue(t, carry):
        for k in range(TOP_K):
            row_copy(t, k, idx_ref[0, 0, t * TOP_K + k]).start(priority=k % 2)
        return carry

    def drain(t, carry):
        for k in range(TOP_K):
            row_copy(t, k, 0).wait()
        return carry

    lax.fori_loop(0, tm, issue, 0)
    lax.fori_loop(0, tm, drain, 0)
    g = g_ref[...]
    y = g[:, 0:1] * buf[0]
    for k in range(1, TOP_K):
        y = y + g[:, k:k + 1] * buf[k]
    m = mod_ref[_group_of((pl.program_id(0) * tm) // sub, subs_per_batch, n_batch)]
    x_new = x_ref[...] + m[5:6] * y
    o_ref[...] = _rms(x_new, NORM_EPS) * fw_ref[...] if final else x_new


def _combine(xt, yb, dest, top_g, mod, final_w, *, sub, subs_per_batch, n_batch, final):
    n, d = xt.shape
    tm = min(sub, 128)
    kern = functools.partial(_combine_kernel, sub=sub, subs_per_batch=subs_per_batch, n_batch=n_batch, final=final)
    row = pl.BlockSpec((tm, d), lambda i: (i, 0))
    return pl.pallas_call(
        kern,
        out_shape=jax.ShapeDtypeStruct((n, d), F32),
        grid=(n // tm,),
        in_specs=[pl.BlockSpec((1, 1, tm * TOP_K), lambda i: (i, 0, 0), memory_space=pltpu.SMEM),
                  row, pl.BlockSpec((tm, LANES), lambda i: (i, 0)),
                  pl.BlockSpec((MOD_ROWS, 6, d), lambda i: (0, 0, 0)), pl.BlockSpec((1, d), lambda i: (0, 0)),
                  pl.BlockSpec(memory_space=pl.ANY)],
        out_specs=row,
        scratch_shapes=[pltpu.VMEM((TOP_K, tm, d), F32), pltpu.SemaphoreType.DMA(())],
        compiler_params=_cparams(("arbitrary",)),
        name="moe_combine",
    )(dest.reshape(n // tm, 1, tm * TOP_K), xt, top_g, mod, final_w.reshape(1, d), yb)


def _pad_cols(w, cuts):
    parts = []
    for start, width, padded in cuts:
        parts.append(w[:, start:start + width])
        if padded > width:
            parts.append(jnp.zeros((w.shape[0], padded - width), w.dtype))
    return jnp.concatenate(parts, axis=1)


def kernel(x, c, ctx, c_ctx, w_mod, b_mod, norm1_w, w_in, gla_gate_w, gla_gate_b, gla_norm_w, gdn_conv_w,
           gdn_a_log, gdn_dt_bias, gdn_norm_w, diff_lambda, diff_norm_w, w_out, norm2_w, router_w, router_b,
           expert_w1, expert_b1, expert_w2, expert_b2, final_norm_w):
    n_batch, s_len, d = x.shape
    l_ctx = ctx.shape[1]
    depth = w_mod.shape[0]
    t_len = l_ctx + s_len
    n_tok = n_batch * t_len
    subs_per_batch = t_len // l_ctx
    assert s_len % l_ctx == 0 and l_ctx % CHUNK == 0 and n_batch < MOD_ROWS and l_ctx % 16 == 0
    rows_per_visit = -(-(n_tok * TOP_K // N_EXPERTS * 9 // 8) // MOE_SUB) * MOE_SUB

    xt = jnp.concatenate([ctx, x], axis=1).reshape(n_tok, d)
    cvec = jnp.concatenate([c, c_ctx[None, :], jnp.zeros((MOD_ROWS - n_batch - 1, d), F32)], axis=0)
    mod_all = _modulation(cvec, w_mod, b_mod).reshape(depth, MOD_ROWS, 6, d)
    cos_t, sa_t, sb_t = _rope_tables(t_len, l_ctx)
    perm = _perm_matrix()
    a_w = GLA_HEADS * (2 * GLA_DK + 2 * GLA_DV)
    b_w = GDN_HEADS * (2 * GDN_DK + 2 * GDN_DV)
    c_w = DIFF_HEADS * (4 * DIFF_DH + DIFF_DV)
    glr0, b0 = a_w, a_w + 2 * GLA_RANK
    ba0 = b0 + b_w
    c0 = ba0 + 4 * GDN_HEADS
    cuts = ((0, a_w, a_w), (b0, b_w, b_w), (c0, c_w, c_w), (glr0, 2 * GLA_RANK, LANES), (ba0, 4 * GDN_HEADS, LANES))
    assert (a_w, a_w + b_w, a_w + b_w + c_w) == (B_Q, C_Q, A_GLR)

    for l in range(depth):
        lambda_init = 0.8 - 0.6 * math.exp(-0.3 * l)
        mod = mod_all[l]
        w_pad = _pad_cols(w_in[l], cuts).astype(BF16)
        p = _in_proj(xt, norm1_w[l], mod, w_pad, sub=l_ctx, subs_per_batch=subs_per_batch, n_batch=n_batch)

        gw = gla_gate_w[l]
        gw_pad = jnp.zeros((2, LANES, GLA_HEADS * GLA_DK), F32)
        gw_pad = gw_pad.at[0, 0:GLA_RANK].set(gw[0]).at[1, GLA_RANK:2 * GLA_RANK].set(gw[1])
        oa = _gla(p, gw_pad, gla_gate_b[l].reshape(2, 1, -1), n_batch=n_batch, t_len=t_len, l_ctx=l_ctx)

        ad = jnp.zeros((8, LANES), F32)
        ad = ad.at[0, 2 * GDN_HEADS:4 * GDN_HEADS].set(gdn_a_log[l].reshape(-1))
        ad = ad.at[1, 2 * GDN_HEADS:4 * GDN_HEADS].set(gdn_dt_bias[l].reshape(-1))
        ob = _gdn(p, gdn_conv_w[l], ad, n_batch=n_batch, t_len=t_len, l_ctx=l_ctx)

        qk = _rope(p, cos_t, sa_t, sb_t, tm=l_ctx)
        yd = _diff_attn(qk, p, diff_lambda[l], diff_norm_w[l], n_batch=n_batch, t_len=t_len, l_ctx=l_ctx,
                        lambda_init=lambda_init)

        rw_pad = jnp.zeros((d, LANES), F32).at[:, :N_EXPERTS].set(router_w[l])
        rb_pad = jnp.full((1, LANES), -1e30, F32).at[0, :N_EXPERTS].set(router_b[l])
        xt, h2, top_e, top_g = _out_proj(oa, ob, yd, p, gla_norm_w[l], gdn_norm_w[l], w_out[l].astype(BF16), xt, mod,
                                         norm2_w[l], rw_pad, rb_pad, tm=l_ctx, subs_per_batch=subs_per_batch,
                                         n_batch=n_batch)

        dest, tail, vis_e, vis_n, n_used, n_vis = _routing(top_e[:, :TOP_K], n_tok, rows_per_visit)
        xs = _dispatch(h2, dest, tail, n_rows=n_vis * rows_per_visit, sub=l_ctx)
        yb = _moe_experts(xs, vis_e, vis_n, n_used, expert_w1, _deinterleave_bias(expert_b1[l]),
                          expert_w2, expert_b2[l], perm, layer=l, rows_per_visit=rows_per_visit)
        xt = _combine(xt, yb, dest, top_g, mod, final_norm_w, sub=l_ctx, subs_per_batch=subs_per_batch,
                      n_batch=n_batch, final=(l == depth - 1))
    return xt.reshape(n_batch, t_len, d)[:, l_ctx:, :]
```
